```python
import jax
import jax.numpy as jnp
from jax import lax
import numpy as np

D_MODEL = 2048
BATCH = 1
SEQ = 8192
DEPTH = 4

HEAD_DIM = 128
D_MIX = D_MODEL
W_A = D_MIX // 4
W_B = D_MIX // 4
W_C = D_MIX // 4
W_D = D_MIX - W_A - W_B - W_C
H_A = W_A // HEAD_DIM
H_B = W_B // HEAD_DIM
G_C = W_C // HEAD_DIM
H_D = W_D // HEAD_DIM
LIN_CHUNK = 64
SGU_CHUNK = 128
GRID_W = 64
NA_ROWS = 8
NA_COLS = 16
CONV_W = 3
D_FF = 5504
EPS = 1e-6
NEG_BIG = -1e30

SPLIT_SIZES = (W_A, W_A, W_A, W_A, 2 * H_A, 2 * H_A,
               W_B, W_B, W_B, W_B, W_B,
               W_C, W_C,
               W_D, W_D, W_D)
D_IN = sum(SPLIT_SIZES)
SPLIT_POINTS = tuple(int(p) for p in np.cumsum(SPLIT_SIZES)[:-1])

kernel_name = "hybrid_parallel_mixer_encoder"

F32 = jnp.float32


def _rms_norm(x, w):
    xf = x.astype(F32)
    y = xf * lax.rsqrt(jnp.mean(xf * xf, axis=-1, keepdims=True) + EPS)
    return (y * w.astype(F32)).astype(x.dtype)


def _rms_heads(x, w):
    B, T, W = x.shape
    xh = x.astype(F32).reshape(B, T, W // HEAD_DIM, HEAD_DIM)
    xh = xh * lax.rsqrt(jnp.mean(xh * xh, axis=-1, keepdims=True) + EPS)
    return xh.reshape(B, T, W) * w.astype(F32)


def _layer_norm(x, w, b):
    xf = x.astype(F32)
    mu = jnp.mean(xf, axis=-1, keepdims=True)
    xc = xf - mu
    var = jnp.mean(xc * xc, axis=-1, keepdims=True)
    return xc * lax.rsqrt(var + EPS) * w.astype(F32) + b.astype(F32)


def _head_layer_norm(h, w):
    mu = jnp.mean(h, axis=-1, keepdims=True)
    hc = h - mu
    var = jnp.mean(hc * hc, axis=-1, keepdims=True)
    return hc * lax.rsqrt(var + EPS) * w.astype(F32)


def _dwconv3(x, w, b):
    xp = jnp.pad(x, ((0, 0), (1, 1), (0, 0)))
    return xp[:, :-2] * w[0] + xp[:, 1:-1] * w[1] + xp[:, 2:] * w[2] + b


def _heads(x, n):
    B, T, _ = x.shape
    return x.reshape(B, T, n, -1).transpose(0, 2, 1, 3)


def _merge(x):
    B, H, T, d = x.shape
    return x.transpose(0, 2, 1, 3).reshape(B, T, H * d)


def _flip(a):
    return jnp.flip(a, axis=2)


def _to_chunks(a, L):
    B, H, T = a.shape[:3]
    a = a.reshape(B, H, T // L, L, *a.shape[3:])
    return jnp.moveaxis(a, 2, 0)


def _from_chunks(a):
    a = jnp.moveaxis(a, 0, 2)
    return a.reshape(a.shape[0], a.shape[1], a.shape[2] * a.shape[3], *a.shape[4:])


def _mlstm_chunkwise(q, k, v, log_i, log_f):
    B, H, T, d = q.shape
    L = LIN_CHUNK
    qc, kc, vc, ic = _to_chunks(q, L), _to_chunks(k, L), _to_chunks(v, L), _to_chunks(log_i, L)
    bc = jnp.cumsum(_to_chunks(log_f, L), axis=3)
    causal = jnp.tril(jnp.ones((L, L), dtype=bool))

    def step(carry, xs):
        C, n, m = carry
        qj, kj, vj, ij, bj = xs
        dmat = jnp.where(causal, bj[..., :, None] - bj[..., None, :] + ij[..., None, :], NEG_BIG)
        inter = bj + m[..., None]
        m_t = jnp.maximum(inter, jnp.max(dmat, axis=-1))
        w_intra = jnp.where(causal, jnp.exp(dmat - m_t[..., None]), 0.0)
        w_inter = jnp.exp(inter - m_t)
        s = jnp.einsum('bhtd,bhsd->bhts', qj, kj) * w_intra
        num = jnp.einsum('bhts,bhsv->bhtv', s, vj) + w_inter[..., None] * jnp.einsum('bhvk,bhtk->bhtv', C, qj)
        den = jnp.sum(s, axis=-1) + w_inter * jnp.einsum('bhk,bhtk->bht', n, qj)
        h = num / jnp.maximum(jnp.abs(den), jnp.exp(-m_t))[..., None]
        b_last = bj[..., -1]
        a = ij + b_last[..., None] - bj
        m_new = jnp.maximum(b_last + m, jnp.max(a, axis=-1))
        decay = jnp.exp(b_last + m - m_new)
        wk = jnp.exp(a - m_new[..., None])
        C = decay[..., None, None] * C + jnp.einsum('bhs,bhsv,bhsk->bhvk', wk, vj, kj)
        n = decay[..., None] * n + jnp.einsum('bhs,bhsk->bhk', wk, kj)
        return (C, n, m_new), h

    init = (jnp.zeros((B, H, d, d), F32), jnp.zeros((B, H, d), F32), jnp.zeros((B, H), F32))
    _, h = lax.scan(step, init, (qc, kc, vc, ic, bc))
    return _from_chunks(h)


def _gla_chunkwise(q, k, v, logf):
    B, H, T, dk = q.shape
    dv = v.shape[-1]
    L = LIN_CHUNK
    qc, kc, vc = _to_chunks(q, L), _to_chunks(k, L), _to_chunks(v, L)
    bc = jnp.cumsum(_to_chunks(logf, L), axis=3)
    causal = jnp.tril(jnp.ones((L, L), dtype=bool))[:, :, None]

    def step(S, xs):
        qj, kj, vj, bj = xs
        diff = bj[:, :, :, None, :] - bj[:, :, None, :, :]
        decay = jnp.where(causal, jnp.exp(jnp.where(causal, diff, 0.0)), 0.0)
        scores = jnp.einsum('bhtk,bhsk,bhtsk->bhts', qj, kj, decay)
        out = jnp.einsum('bhts,bhsv->bhtv', scores, vj) + jnp.einsum('bhtk,bhkv->bhtv', qj * jnp.exp(bj), S)
        b_last = bj[:, :, -1]
        S = jnp.exp(b_last)[..., None] * S + jnp.einsum('bhsk,bhsv->bhkv', kj * jnp.exp(b_last[:, :, None] - bj), vj)
        return S, out

    _, out = lax.scan(step, jnp.zeros((B, H, dk, dv), F32), (qc, kc, vc, bc))
    return _from_chunks(out)


def _mlstm_mixer(q, k, v, o, i_pre, f_pre, conv_w, conv_b, i_bias, f_bias, norm_w):
    B, T, _ = q.shape
    qk = jax.nn.silu(_dwconv3(jnp.concatenate([q, k], axis=-1), conv_w, conv_b))
    q, k = jnp.split(qk.astype(F32), 2, axis=-1)
    qh = _heads(q, H_A)
    kh = _heads(k, H_A) * (HEAD_DIM ** -0.5)
    vh = _heads(v.astype(F32), H_A)
    log_i = (i_pre.astype(F32).reshape(B, T, 2, H_A) + i_bias.astype(F32)).transpose(2, 0, 3, 1)
    log_f = jax.nn.log_sigmoid(f_pre.astype(F32).reshape(B, T, 2, H_A) + f_bias.astype(F32)).transpose(2, 0, 3, 1)
    h = (_mlstm_chunkwise(qh, kh, vh, log_i[0], log_f[0])
         + _flip(_mlstm_chunkwise(_flip(qh), _flip(kh), _flip(vh), _flip(log_i[1]), _flip(log_f[1]))))
    h = _head_layer_norm(h, norm_w.reshape(H_A, 1, HEAD_DIM))
    return _merge(h) * jax.nn.sigmoid(o.astype(F32))


def _hgrn2_mixer(q, f_fwd, f_bwd, i, g, lb, f_bias, norm_w):
    qh = _heads(jax.nn.silu(q.astype(F32)), H_B)
    vh = _heads(i.astype(F32), H_B)

    def gates(z):
        z = z.astype(F32)
        f = lb + (1.0 - lb) * jax.nn.sigmoid(z)
        key = (1.0 - lb) * jax.nn.sigmoid(-z)
        return _heads(jnp.log(f), H_B), _heads(key, H_B)

    lf_f, k_f = gates(f_fwd + f_bias[0])
    lf_b, k_b = gates(f_bwd + f_bias[1])
    o = (_gla_chunkwise(qh, k_f, vh, lf_f)
         + _flip(_gla_chunkwise(_flip(qh), _flip(k_b), _flip(vh), _flip(lf_b))))
    o = _rms_heads(_merge(o), norm_w)
    return o * jax.nn.silu(g.astype(F32))


def _sgu_mixer(u, v, ln_w, ln_b, w_s, b_s):
    B, T, _ = u.shape
    u = jax.nn.gelu(u.astype(F32), approximate=False)
    v = _layer_norm(jax.nn.gelu(v.astype(F32), approximate=False), ln_w, ln_b)
    vc = v.reshape(B, T // SGU_CHUNK, SGU_CHUNK, G_C, HEAD_DIM)
    mixed = (jnp.einsum('gts,bnsgc->bntgc', w_s.astype(F32), vc)
             + b_s.astype(F32).T[None, None, :, :, None])
    return u * mixed.reshape(B, T, W_C)


def _na_mixer(q, k, v, rel_bias):
    B, T, _ = q.shape
    rows = T // GRID_W
    kr = min(NA_ROWS, rows)

    def grid(a):
        return _heads(a, H_D).reshape(B, H_D, rows, GRID_W, HEAD_DIM)

    qg, kg, vg = grid(q), grid(k), grid(v)
    cols = np.arange(GRID_W)
    col_start = np.clip(cols - NA_COLS // 2, 0, GRID_W - NA_COLS)
    col_idx = col_start[:, None] + np.arange(NA_COLS)[None, :]
    col_off = col_idx - cols[:, None] + (NA_COLS - 1)
    scale = HEAD_DIM ** -0.5

    def one_row(r):
        rs = jnp.clip(r - kr // 2, 0, rows - kr)
        k_win = lax.dynamic_slice_in_dim(kg, rs, kr, axis=2)[:, :, :, col_idx]
        v_win = lax.dynamic_slice_in_dim(vg, rs, kr, axis=2)[:, :, :, col_idx]
        q_row = lax.dynamic_index_in_dim(qg, r, axis=2, keepdims=False)
        s = jnp.einsum('bhcd,bhrcjd->bhcrj', q_row, k_win).astype(F32) * scale
        row_off = rs + jnp.arange(kr) - r + (NA_ROWS - 1)
        bias = rel_bias[:, row_off][:, :, col_off].transpose(0, 2, 1, 3)
        s = (s + bias.astype(F32)[None]).reshape(B, H_D, GRID_W, kr * NA_COLS)
        p = jax.nn.softmax(s, axis=-1).reshape(B, H_D, GRID_W, kr, NA_COLS)
        return jnp.einsum('bhcrj,bhrcjd->bhcd', p, v_win.astype(F32))

    out = lax.map(one_row, jnp.arange(rows))
    return out.transpose(1, 2, 0, 3, 4).reshape(B, H_D, T, HEAD_DIM)


def _conv_glu_ffn(h, w_up, conv_w, conv_b, w_down):
    z = _dwconv3(h @ w_up, conv_w, conv_b)
    a, g = jnp.split(z, 2, axis=-1)
    return (jax.nn.silu(g) * a) @ w_down


def setup_inputs(seed: int = 0) -> dict:
    key = jax.random.key(seed)
    ks = jax.random.split(key, 24)
    nrm = lambda k, s: jax.random.normal(k, s, F32)
    return {
        "x": nrm(ks[0], (BATCH, SEQ, D_MODEL)),
        "norm1_w": 1.0 + 0.02 * nrm(ks[1], (DEPTH, D_MODEL)),
        "w_in": nrm(ks[2], (DEPTH, D_MODEL, D_IN)) * D_MODEL ** -0.5,
        "mlstm_conv_w": nrm(ks[3], (DEPTH, CONV_W, 2 * W_A)) * CONV_W ** -0.5,
        "mlstm_conv_b": 0.02 * nrm(ks[4], (DEPTH, 2 * W_A)),
        "mlstm_i_bias": 0.1 * nrm(ks[5], (DEPTH, 2, H_A)),
        "mlstm_f_bias": 3.0 + 3.0 * jax.random.uniform(ks[6], (DEPTH, 2, H_A), F32),
        "mlstm_norm_w": 1.0 + 0.02 * nrm(ks[7], (DEPTH, W_A)),
        "hgrn_lb_logits": 0.02 * nrm(ks[8], (DEPTH, W_B)),
        "hgrn_f_bias": 0.1 * nrm(ks[9], (DEPTH, 2, W_B)),
        "hgrn_norm_w": 1.0 + 0.02 * nrm(ks[10], (DEPTH, W_B)),
        "sgu_ln_w": 1.0 + 0.02 * nrm(ks[11], (DEPTH, W_C)),
        "sgu_ln_b": 0.02 * nrm(ks[12], (DEPTH, W_C)),
        "sgu_w": nrm(ks[13], (DEPTH, G_C, SGU_CHUNK, SGU_CHUNK)) * SGU_CHUNK ** -0.5,
        "sgu_b": 1.0 + 0.02 * nrm(ks[14], (DEPTH, G_C, SGU_CHUNK)),
        "na_rel_bias": 0.1 * nrm(ks[15], (DEPTH, H_D, 2 * NA_ROWS - 1, 2 * NA_COLS - 1)),
        "out_norm_w": 1.0 + 0.02 * nrm(ks[16], (DEPTH, W_C + W_D)),
        "w_out": nrm(ks[17], (DEPTH, D_MIX, D_MODEL)) * D_MIX ** -0.5,
        "norm2_w": 1.0 + 0.02 * nrm(ks[18], (DEPTH, D_MODEL)),
        "ffn_w_up": nrm(ks[19], (DEPTH, D_MODEL, 2 * D_FF)) * D_MODEL ** -0.5,
        "ffn_conv_w": nrm(ks[20], (DEPTH, CONV_W, 2 * D_FF)) * CONV_W ** -0.5,
        "ffn_conv_b": 0.02 * nrm(ks[21], (DEPTH, 2 * D_FF)),
        "ffn_w_down": nrm(ks[22], (DEPTH, D_FF, D_MODEL)) * D_FF ** -0.5,
        "final_norm_w": 1.0 + 0.02 * nrm(ks[23], (D_MODEL,)),
    }


def reference(x, norm1_w, w_in, mlstm_conv_w, mlstm_conv_b, mlstm_i_bias, mlstm_f_bias,
              mlstm_norm_w, hgrn_lb_logits, hgrn_f_bias, hgrn_norm_w, sgu_ln_w, sgu_ln_b,
              sgu_w, sgu_b, na_rel_bias, out_norm_w, w_out, norm2_w, ffn_w_up, ffn_conv_w,
              ffn_conv_b, ffn_w_down, final_norm_w):
    dt = x.dtype
    lb_all = jax.nn.softmax(hgrn_lb_logits.astype(F32), axis=0)
    lower_bounds = jnp.cumsum(lb_all, axis=0) - lb_all[0]
    for l in range(DEPTH):
        h = _rms_norm(x, norm1_w[l])
        proj = h @ w_in[l]
        (a_q, a_k, a_v, a_o, a_i, a_f, b_q, b_ff, b_fb, b_i, b_g,
         c_u, c_v, d_q, d_k, d_v) = jnp.split(proj, SPLIT_POINTS, axis=-1)
        y_a = _mlstm_mixer(a_q, a_k, a_v, a_o, a_i, a_f, mlstm_conv_w[l], mlstm_conv_b[l],
                           mlstm_i_bias[l], mlstm_f_bias[l], mlstm_norm_w[l])
        y_b = _hgrn2_mixer(b_q, b_ff, b_fb, b_i, b_g, lower_bounds[l], hgrn_f_bias[l], hgrn_norm_w[l])
        y_c = _rms_heads(_sgu_mixer(c_u, c_v, sgu_ln_w[l], sgu_ln_b[l], sgu_w[l], sgu_b[l]),
                         out_norm_w[l, :W_C])
        y_d = _rms_heads(_merge(_na_mixer(d_q, d_k, d_v, na_rel_bias[l])), out_norm_w[l, W_C:])
        y = jnp.concatenate([y_a.astype(dt), y_b.astype(dt), y_c.astype(dt), y_d.astype(dt)], axis=-1)
        x = x + y @ w_out[l]
        h = _rms_norm(x, norm2_w[l])
        x = x + _conv_glu_ffn(h, ffn_w_up[l], ffn_conv_w[l], ffn_conv_b[l], ffn_w_down[l])
    return _rms_norm(x, final_norm_w)
```

```python
import functools

import numpy as np
import jax
import jax.numpy as jnp
from jax import lax
from jax.experimental import pallas as pl
from jax.experimental.pallas import tpu as pltpu

F32 = jnp.float32
BF16 = jnp.bfloat16
HIGHEST = lax.Precision.HIGHEST

D_MODEL = 2048
HEAD_DIM = 128
N_HEADS = 4
W_GRP = N_HEADS * HEAD_DIM
D_FF = 5504
GRID_W = 64
NA_ROWS = 8
NA_COLS = 16
EPS = 1e-6
NEG_BIG = -1e30

LIN_CHUNK = 64
SGU_CHUNK = 128
GATE_PAD = 128
FF_TILE = 512
D_FF_PAD = 5632
HALO = 8

COL_A_Q, COL_A_K, COL_A_V, COL_A_O = 0, 1, 2, 3
COL_B_Q, COL_B_FF, COL_B_FB, COL_B_I, COL_B_G = 4, 5, 6, 7, 8
COL_C_U, COL_C_V = 9, 10
COL_D_Q, COL_D_K, COL_D_V = 11, 12, 13
N_MAIN = 14 * W_GRP

VMEM_LIMIT = 56 * 1024 * 1024

_NT = (((1,), (1,)), ((), ()))
_TN = (((0,), (0,)), ((), ()))


def _dot(a, b, dims=None, precision=None):
    if dims is None:
        return jnp.dot(a, b, preferred_element_type=F32, precision=precision)
    return lax.dot_general(a, b, dims, preferred_element_type=F32, precision=precision)


def _bdot(a, b, dims=None):
    return _dot(a.astype(BF16), b.astype(BF16), dims)


def _sigmoid(x):
    return jax.nn.sigmoid(x)


def _log_sigmoid(x):
    return jnp.minimum(x, 0.0) - jnp.log1p(jnp.exp(-jnp.abs(x)))


def _params(sem):
    return pltpu.CompilerParams(dimension_semantics=sem, vmem_limit_bytes=VMEM_LIMIT)


def _proj_in_kernel(x_ref, nw_ref, w_ref, wg_ref, proj_ref, gates_ref, h_ref):
    @pl.when(pl.program_id(1) == 0)
    def _():
        xf = x_ref[...]
        ms = jnp.mean(xf * xf, axis=-1, keepdims=True)
        h = (xf * lax.rsqrt(ms + EPS) * nw_ref[...]).astype(BF16)
        h_ref[...] = h
        gates_ref[...] = _dot(h, wg_ref[...])

    proj_ref[...] = _dot(h_ref[...], w_ref[...])


def _proj_in(x, norm_w, w_main, w_gate, tm, tn):
    T, D = x.shape
    N = w_main.shape[1]
    return pl.pallas_call(
        _proj_in_kernel,
        grid=(T // tm, N // tn),
        in_specs=[
            pl.BlockSpec((tm, D), lambda i, j: (i, 0)),
            pl.BlockSpec((1, D), lambda i, j: (0, 0)),
            pl.BlockSpec((D, tn), lambda i, j: (0, j)),
            pl.BlockSpec((D, GATE_PAD), lambda i, j: (0, 0)),
        ],
        out_specs=[
            pl.BlockSpec((tm, tn), lambda i, j: (i, j)),
            pl.BlockSpec((tm, GATE_PAD), lambda i, j: (i, 0)),
        ],
        out_shape=[
            jax.ShapeDtypeStruct((T, N), F32),
            jax.ShapeDtypeStruct((T, GATE_PAD), F32),
        ],
        scratch_shapes=[pltpu.VMEM((tm, D), BF16)],
        compiler_params=_params(("arbitrary", "arbitrary")),
        name="proj_in",
    )(x, norm_w.reshape(1, D), w_main, w_gate)


def _causal_mask(L, reverse):
    t = np.arange(L)[:, None]
    s = np.arange(L)[None, :]
    return ((s >= t) if reverse else (s <= t)).astype(np.float32)


def _shift_rows(x, prev_row, next_row):
    n = x.shape[0]
    rows = lax.broadcasted_iota(jnp.int32, (n, 1), 0)
    xm1 = jnp.where(rows == 0, prev_row, pltpu.roll(x, 1, 0))
    xp1 = jnp.where(rows == n - 1, next_row, pltpu.roll(x, n - 1, 0))
    return xm1, xp1


def _dwconv3(x, prev_row, next_row, w3, b):
    xm1, xp1 = _shift_rows(x, prev_row, next_row)
    return xm1 * w3[0:1] + x * w3[1:2] + xp1 * w3[2:3] + b


def _halo_specs(G, T, width, col, rev):
    nb = T // G
    per = G // HALO
    last = T // HALO - 1

    def blk(c):
        return (nb - 1 - c) if rev else c

    prev = pl.BlockSpec((HALO, width), lambda c: (jnp.maximum(blk(c) * per - 1, 0), col))
    nxt = pl.BlockSpec((HALO, width), lambda c: (jnp.minimum((blk(c) + 1) * per, last), col))
    return prev, nxt


def _mlstm_kernel(*refs, reverse, G, L):
    if reverse:
        (q_ref, qp_ref, qn_ref, k_ref, kp_ref, kn_ref, v_ref, g_ref, gt_ref, gb_ref, gbt_ref,
         cw_ref, cb_ref, mask_ref, hf_ref, o_ref, nw_ref, out_ref,
         qc_ref, kc_ref, ct_ref, m_ref) = refs
    else:
        (q_ref, qp_ref, qn_ref, k_ref, kp_ref, kn_ref, v_ref, g_ref, gt_ref, gb_ref, gbt_ref,
         cw_ref, cb_ref, mask_ref, out_ref,
         qc_ref, kc_ref, ct_ref, m_ref) = refs
    c = pl.program_id(0)
    nblk = pl.num_programs(0)
    blk = (nblk - 1 - c) if reverse else c

    @pl.when(c == 0)
    def _():
        ct_ref[...] = jnp.zeros_like(ct_ref)
        m_ref[...] = jnp.zeros_like(m_ref)

    has_prev = (blk > 0).astype(F32)
    has_next = (blk < nblk - 1).astype(F32)
    cw = cw_ref[...]
    cb = cb_ref[...]
    zq = _dwconv3(q_ref[...], qp_ref[HALO - 1:HALO, :] * has_prev, qn_ref[0:1, :] * has_next,
                  cw[:, :W_GRP], cb[:, :W_GRP])
    qc_ref[...] = zq * _sigmoid(zq)
    zk = _dwconv3(k_ref[...], kp_ref[HALO - 1:HALO, :] * has_prev, kn_ref[0:1, :] * has_next,
                  cw[:, W_GRP:], cb[:, W_GRP:])
    kc_ref[...] = zk * _sigmoid(zk) * (HEAD_DIM ** -0.5)

    maskf = mask_ref[...]
    mask = maskf > 0.5
    ones_col = (lax.broadcasted_iota(jnp.int32, (L, HEAD_DIM), 1) == 0).astype(F32)
    d = 1 if reverse else 0
    nsub = G // L

    def chunk(j, carry):
        jj = (nsub - 1 - j) if reverse else j
        r0 = pl.multiple_of(jj * L, L)
        rows = pl.ds(r0, L)
        gch = g_ref[rows, :] + gb_ref[...]
        b_cols = _dot(maskf, _log_sigmoid(gch), precision=HIGHEST)
        gt = gt_ref[jj] + gbt_ref[...]
        b_rows = _dot(_log_sigmoid(gt), maskf, _NT, precision=HIGHEST)
        for h in range(N_HEADS):
            ci = d * N_HEADS + h
            cf = 2 * N_HEADS + ci
            hc = slice(h * HEAD_DIM, (h + 1) * HEAD_DIM)
            i_col, b_col = gch[:, ci:ci + 1], b_cols[:, cf:cf + 1]
            i_row, b_row = gt[ci:ci + 1, :], b_rows[cf:cf + 1, :]
            qh = qc_ref[rows, hc]
            kh = kc_ref[rows, hc]
            v_aug = jnp.concatenate([v_ref[rows, hc], ones_col], axis=1)
            ct = ct_ref[h]
            m_prev = m_ref[h][0:1, 0:1]

            dmat = jnp.where(mask, b_col - b_row + i_row, NEG_BIG)
            inter = b_col + m_prev
            m_t = jnp.maximum(inter, jnp.max(dmat, axis=1, keepdims=True))
            w_intra = jnp.where(mask, jnp.exp(dmat - m_t), 0.0)
            w_inter = jnp.exp(inter - m_t)
            s = _bdot(qh, kh, _NT) * w_intra
            num = _bdot(s, v_aug) + w_inter * _bdot(qh, ct)
            den = num[:, HEAD_DIM:HEAD_DIM + 1]
            hh = num[:, :HEAD_DIM] / jnp.maximum(jnp.abs(den), jnp.exp(-m_t))

            b_last = b_col[0:1] if reverse else b_col[L - 1:L]
            a_col = i_col + b_last - b_col
            m_new = jnp.maximum(b_last + m_prev, jnp.max(a_col, axis=0, keepdims=True))
            decay = jnp.exp(b_last + m_prev - m_new)
            wk = jnp.exp(a_col - m_new)
            ct_ref[h] = decay * ct + _bdot(kh, wk * v_aug, _TN)
            m_ref[h] = jnp.broadcast_to(m_new, m_ref.shape[1:])

            if reverse:
                tot = hf_ref[rows, hc] + hh
                mu = jnp.mean(tot, axis=-1, keepdims=True)
                tc = tot - mu
                var = jnp.mean(tc * tc, axis=-1, keepdims=True)
                y = tc * lax.rsqrt(var + EPS) * nw_ref[:, hc] * _sigmoid(o_ref[rows, hc])
                out_ref[rows, hc] = y.astype(out_ref.dtype)
            else:
                out_ref[rows, hc] = hh
        return carry

    lax.fori_loop(0, nsub, chunk, 0)


def _mlstm_dir(proj, gates, gates_t, gb, gbt, conv_w, conv_b, reverse, G, hf=None, norm_w=None):
    T = proj.shape[0]
    L = LIN_CHUNK
    nb = T // G

    def blk(c):
        return (nb - 1 - c) if reverse else c

    def col_spec(col):
        return pl.BlockSpec((G, W_GRP), lambda c: (blk(c), col))

    qp, qn = _halo_specs(G, T, W_GRP, COL_A_Q, reverse)
    kp, kn = _halo_specs(G, T, W_GRP, COL_A_K, reverse)
    const2 = lambda shape: pl.BlockSpec(shape, lambda c: (0, 0))
    in_specs = [
        col_spec(COL_A_Q), qp, qn, col_spec(COL_A_K), kp, kn, col_spec(COL_A_V),
        pl.BlockSpec((G, GATE_PAD), lambda c: (blk(c), 0)),
        pl.BlockSpec((G // L, 4 * N_HEADS, L), lambda c: (blk(c), 0, 0)),
        const2((1, GATE_PAD)), const2((4 * N_HEADS, 1)),
        const2((3, 2 * W_GRP)), const2((1, 2 * W_GRP)), const2((L, L)),
    ]
    args = [proj, proj, proj, proj, proj, proj, proj, gates, gates_t, gb, gbt,
            conv_w, conv_b.reshape(1, -1), jnp.asarray(_causal_mask(L, reverse))]
    if reverse:
        in_specs += [pl.BlockSpec((G, W_GRP), lambda c: (blk(c), 0)), col_spec(COL_A_O),
                     const2((1, W_GRP))]
        args += [hf, proj, norm_w.reshape(1, -1)]
    return pl.pallas_call(
        functools.partial(_mlstm_kernel, reverse=reverse, G=G, L=L),
        grid=(nb,),
        in_specs=in_specs,
        out_specs=pl.BlockSpec((G, W_GRP), lambda c: (blk(c), 0)),
        out_shape=jax.ShapeDtypeStruct((T, W_GRP), BF16 if reverse else F32),
        scratch_shapes=[
            pltpu.VMEM((G, W_GRP), F32), pltpu.VMEM((G, W_GRP), F32),
            pltpu.VMEM((N_HEADS, HEAD_DIM, 2 * HEAD_DIM), F32),
            pltpu.VMEM((N_HEADS, 8, 128), F32),
        ],
        compiler_params=_params(("arbitrary",)),
        name="mlstm_bwd" if reverse else "mlstm_fwd",
    )(*args)


def _mlstm(proj, gates, conv_w, conv_b, i_bias, f_bias, norm_w, G):
    T = proj.shape[0]
    L = LIN_CHUNK
    ng = 4 * N_HEADS
    gbias = jnp.concatenate([i_bias.reshape(-1), f_bias.reshape(-1)]).astype(F32)
    gb = jnp.pad(gbias, (0, GATE_PAD - ng)).reshape(1, GATE_PAD)
    gbt = gbias.reshape(ng, 1)
    gates_t = gates[:, :ng].reshape(T // L, L, ng).transpose(0, 2, 1)
    hf = _mlstm_dir(proj, gates, gates_t, gb, gbt, conv_w, conv_b, False, G)
    return _mlstm_dir(proj, gates, gates_t, gb, gbt, conv_w, conv_b, True, G, hf, norm_w)


def _hgrn_tables(L, reverse):
    n_lev = int(np.log2(L))
    t = np.arange(L)
    base = _causal_mask(L, reverse)
    pall = [base]
    lmask = []
    for lev in range(n_lev):
        h = 1 << lev
        blk = t // (2 * h)
        upper = (t // h) % 2 == 1
        anchor = blk * 2 * h + (h if reverse else h - 1)
        pall.append(base[anchor])
        is_q = ~upper if reverse else upper
        same = blk[:, None] == blk[None, :]
        lmask.append((same & is_q[:, None] & (~is_q)[None, :]).astype(np.float32))
    lmask.append(np.eye(L, dtype=np.float32))
    return np.stack(pall).astype(np.float32), np.stack(lmask)


def _hgrn_kernel(*refs, reverse, G, L):
    if reverse:
        (q_ref, f_ref, i_ref, lb_ref, fb_ref, pall_ref, lmask_ref, of_ref, g_ref, nw_ref,
         out_ref, st_ref) = refs
    else:
        (q_ref, f_ref, i_ref, lb_ref, fb_ref, pall_ref, lmask_ref, out_ref, st_ref) = refs
    c = pl.program_id(0)

    @pl.when(c == 0)
    def _():
        st_ref[...] = jnp.zeros_like(st_ref)

    n_lev = pall_ref.shape[0] - 1
    nsub = G // L
    pall = pall_ref[...].reshape((n_lev + 1) * L, L)

    def chunk(j, carry):
        jj = (nsub - 1 - j) if reverse else j
        r0 = pl.multiple_of(jj * L, L)
        rows = pl.ds(r0, L)
        lb = lb_ref[...]
        z = f_ref[rows, :] + fb_ref[...]
        logf = jnp.log(lb + (1.0 - lb) * _sigmoid(z))
        key = (1.0 - lb) * _sigmoid(-z)
        qs = q_ref[rows, :]
        qs = qs * _sigmoid(qs)
        cums = _dot(pall, logf, precision=HIGHEST)
        b = cums[:L]
        for h in range(N_HEADS):
            hc = slice(h * HEAD_DIM, (h + 1) * HEAD_DIM)
            qh, kh, bh = qs[:, hc], key[:, hc], b[:, hc]
            vh = i_ref[rows, hc]
            scores = lmask_ref[n_lev] * _bdot(qh, kh, _NT)
            for lev in range(n_lev):
                anchor = cums[(lev + 1) * L:(lev + 2) * L, hc]
                qe = qh * jnp.exp(jnp.minimum(bh - anchor, 0.0))
                ke = kh * jnp.exp(jnp.minimum(anchor - bh, 0.0))
                scores = scores + lmask_ref[lev] * _bdot(qe, ke, _NT)
            st = st_ref[h]
            o = _bdot(scores, vh) + _bdot(qh * jnp.exp(bh), st, _NT)
            b_last = bh[0:1] if reverse else bh[L - 1:L]
            st_ref[h] = st * jnp.exp(b_last) + _bdot(vh, kh * jnp.exp(b_last - bh), _TN)
            if reverse:
                tot = of_ref[rows, hc] + o
                ms = jnp.mean(tot * tot, axis=-1, keepdims=True)
                gg = g_ref[rows, hc]
                y = tot * lax.rsqrt(ms + EPS) * nw_ref[:, hc] * (gg * _sigmoid(gg))
                out_ref[rows, hc] = y.astype(out_ref.dtype)
            else:
                out_ref[rows, hc] = o
        return carry

    lax.fori_loop(0, nsub, chunk, 0)


def _hgrn_dir(proj, lb, fb, reverse, G, of=None, norm_w=None):
    T = proj.shape[0]
    L = LIN_CHUNK
    nb = T // G
    pall, lmask = _hgrn_tables(L, reverse)

    def blk(c):
        return (nb - 1 - c) if reverse else c

    def col_spec(col):
        return pl.BlockSpec((G, W_GRP), lambda c: (blk(c), col))

    const2 = lambda shape: pl.BlockSpec(shape, lambda c: (0, 0))
    const3 = lambda shape: pl.BlockSpec(shape, lambda c: (0, 0, 0))
    in_specs = [col_spec(COL_B_Q), col_spec(COL_B_FB if reverse else COL_B_FF), col_spec(COL_B_I),
                const2((1, W_GRP)), const2((1, W_GRP)), const3(pall.shape), const3(lmask.shape)]
    args = [proj, proj, proj, lb.reshape(1, -1), fb.reshape(1, -1), jnp.asarray(pall), jnp.asarray(lmask)]
    if reverse:
        in_specs += [pl.BlockSpec((G, W_GRP), lambda c: (blk(c), 0)), col_spec(COL_B_G), const2((1, W_GRP))]
        args += [of, proj, norm_w.reshape(1, -1)]
    return pl.pallas_call(
        functools.partial(_hgrn_kernel, reverse=reverse, G=G, L=L),
        grid=(nb,),
        in_specs=in_specs,
        out_specs=pl.BlockSpec((G, W_GRP), lambda c: (blk(c), 0)),
        out_shape=jax.ShapeDtypeStruct((T, W_GRP), BF16 if reverse else F32),
        scratch_shapes=[pltpu.VMEM((N_HEADS, HEAD_DIM, HEAD_DIM), F32)],
        compiler_params=_params(("arbitrary",)),
        name="hgrn_bwd" if reverse else "hgrn_fwd",
    )(*args)


def _hgrn(proj, lb, f_bias, norm_w, G):
    of = _hgrn_dir(proj, lb, f_bias[0], False, G)
    return _hgrn_dir(proj, lb, f_bias[1], True, G, of, norm_w)


def _gelu(x):
    return 0.5 * x * (1.0 + lax.erf(x * (2.0 ** -0.5)))


def _sgu_kernel(u_ref, v_ref, lnw_ref, lnb_ref, ws_ref, bs_ref, nw_ref, out_ref, *, G):
    v = _gelu(v_ref[...])
    mu = jnp.mean(v, axis=-1, keepdims=True)
    vc = v - mu
    var = jnp.mean(vc * vc, axis=-1, keepdims=True)
    vn = (vc * lax.rsqrt(var + EPS) * lnw_ref[...] + lnb_ref[...]).astype(BF16)
    for n in range(G // SGU_CHUNK):
        rows = slice(n * SGU_CHUNK, (n + 1) * SGU_CHUNK)
        for g in range(N_HEADS):
            hc = slice(g * HEAD_DIM, (g + 1) * HEAD_DIM)
            mixed = _dot(ws_ref[g], vn[rows, hc]) + bs_ref[:, g:g + 1]
            y = _gelu(u_ref[rows, hc]) * mixed
            ms = jnp.mean(y * y, axis=-1, keepdims=True)
            out_ref[rows, hc] = (y * lax.rsqrt(ms + EPS) * nw_ref[:, hc]).astype(out_ref.dtype)


def _sgu(proj, ln_w, ln_b, w_s, b_s, norm_w, G):
    T = proj.shape[0]
    row = lambda a: a.reshape(1, -1)
    const2 = lambda shape: pl.BlockSpec(shape, lambda c: (0, 0))
    return pl.pallas_call(
        functools.partial(_sgu_kernel, G=G),
        grid=(T // G,),
        in_specs=[
            pl.BlockSpec((G, W_GRP), lambda c: (c, COL_C_U)),
            pl.BlockSpec((G, W_GRP), lambda c: (c, COL_C_V)),
            const2((1, W_GRP)), const2((1, W_GRP)),
            pl.BlockSpec((N_HEADS, SGU_CHUNK, SGU_CHUNK), lambda c: (0, 0, 0)),
            const2((SGU_CHUNK, N_HEADS)), const2((1, W_GRP)),
        ],
        out_specs=pl.BlockSpec((G, W_GRP), lambda c: (c, 0)),
        out_shape=jax.ShapeDtypeStruct((T, W_GRP), BF16),
        compiler_params=_params(("arbitrary",)),
        name="sgu",
    )(proj, proj, row(ln_w), row(ln_b), w_s.astype(BF16), b_s.T, row(norm_w))


def _na_bias_table(rel_bias):
    cols = np.arange(GRID_W)
    col_start = np.clip(cols - NA_COLS // 2, 0, GRID_W - NA_COLS)
    cc = np.arange(GRID_W)[None, :]
    valid = (cc >= col_start[:, None]) & (cc < col_start[:, None] + NA_COLS)
    col_off = np.clip(cc - cols[:, None] + (NA_COLS - 1), 0, 2 * NA_COLS - 2)
    e = np.arange(NA_ROWS)[:, None]
    j = np.arange(NA_ROWS)[None, :]
    row_off = j - e + (NA_ROWS - 1)
    g = rel_bias.astype(F32)[:, row_off][:, :, :, col_off]
    g = jnp.where(valid[None, None, None], g, NEG_BIG)
    g = g.transpose(0, 1, 3, 2, 4)
    return g.reshape(g.shape[0], NA_ROWS, GRID_W, NA_ROWS * GRID_W)


def _na_kernel(q_ref, k_ref, v_ref, bias_ref, nw_ref, out_ref, *, RB, n_rows):
    R = pl.program_id(1)
    win = NA_ROWS * GRID_W
    scale = HEAD_DIM ** -0.5

    def one_row(i, carry):
        r = R * RB + i
        rs = jnp.clip(r - NA_ROWS // 2, 0, n_rows - NA_ROWS)
        e = r - rs
        qrows = pl.ds(pl.multiple_of(i * GRID_W, GRID_W), GRID_W)
        krows = pl.ds(pl.multiple_of(rs * GRID_W, GRID_W), win)
        s = _bdot(q_ref[qrows, :], k_ref[krows, :], _NT) * scale + bias_ref[e]
        p = jnp.exp(s - jnp.max(s, axis=-1, keepdims=True))
        o = _bdot(p, v_ref[krows, :]) / jnp.sum(p, axis=-1, keepdims=True)
        ms = jnp.mean(o * o, axis=-1, keepdims=True)
        out_ref[qrows, :] = (o * lax.rsqrt(ms + EPS) * nw_ref[...]).astype(out_ref.dtype)
        return carry

    lax.fori_loop(0, RB, one_row, 0)


def _na(proj, rel_bias, norm_w, RB):
    T = proj.shape[0]
    n_rows = T // GRID_W
    bias = _na_bias_table(rel_bias)
    cq, ck, cv = (c * N_HEADS for c in (COL_D_Q, COL_D_K, COL_D_V))
    return pl.pallas_call(
        functools.partial(_na_kernel, RB=RB, n_rows=n_rows),
        grid=(N_HEADS, n_rows // RB),
        in_specs=[
            pl.BlockSpec((RB * GRID_W, HEAD_DIM), lambda h, r: (r, cq + h)),
            pl.BlockSpec((T, HEAD_DIM), lambda h, r: (0, ck + h)),
            pl.BlockSpec((T, HEAD_DIM), lambda h, r: (0, cv + h)),
            pl.BlockSpec((None, NA_ROWS, GRID_W, NA_ROWS * GRID_W), lambda h, r: (h, 0, 0, 0)),
            pl.BlockSpec((1, HEAD_DIM), lambda h, r: (0, h)),
        ],
        out_specs=pl.BlockSpec((RB * GRID_W, HEAD_DIM), lambda h, r: (r, h)),
        out_shape=jax.ShapeDtypeStruct((T, W_GRP), BF16),
        compiler_params=_params(("arbitrary", "arbitrary")),
        name="na",
    )(proj, proj, proj, bias, norm_w.reshape(1, -1))


def _proj_out_kernel(x_ref, ya_ref, yb_ref, yc_ref, yd_ref, wa_ref, wb_ref, wc_ref, wd_ref, out_ref):
    acc = x_ref[...]
    for y_ref, w_ref in ((ya_ref, wa_ref), (yb_ref, wb_ref), (yc_ref, wc_ref), (yd_ref, wd_ref)):
        acc = acc + _dot(y_ref[...], w_ref[...])
    out_ref[...] = acc


def _proj_out(x, ys, w_out, tm, tn):
    T, D = x.shape
    y_spec = pl.BlockSpec((tm, W_GRP), lambda i, j: (i, 0))
    w_specs = [pl.BlockSpec((W_GRP, tn), functools.partial(lambda i, j, g: (g, j), g=g)) for g in range(4)]
    return pl.pallas_call(
        _proj_out_kernel,
        grid=(T // tm, D // tn),
        in_specs=[pl.BlockSpec((tm, tn), lambda i, j: (i, j))] + [y_spec] * 4 + w_specs,
        out_specs=pl.BlockSpec((tm, tn), lambda i, j: (i, j)),
        out_shape=jax.ShapeDtypeStruct((T, D), F32),
        compiler_params=_params(("arbitrary", "arbitrary")),
        name="proj_out",
    )(x, *ys, w_out, w_out, w_out, w_out)


def _ffn_up_kernel(x_ref, xp_ref, xn_ref, nw_ref, w_ref, cw_ref, cb_ref, out_ref, h_ref, *, tm):
    i = pl.program_id(0)
    nblk = pl.num_programs(0)

    def norm(xf):
        ms = jnp.mean(xf * xf, axis=-1, keepdims=True)
        return xf * lax.rsqrt(ms + EPS) * nw_ref[...]

    @pl.when(pl.program_id(1) == 0)
    def _():
        h_ref[0:HALO, :] = (norm(xp_ref[...]) * (i > 0).astype(F32)).astype(BF16)
        h_ref[HALO:HALO + tm, :] = norm(x_ref[...]).astype(BF16)
        h_ref[HALO + tm:, :] = (norm(xn_ref[...]) * (i < nblk - 1).astype(F32)).astype(BF16)

    z = _dot(h_ref[...], w_ref[...])
    n = z.shape[0]
    zm1 = pltpu.roll(z, 1, 0)[HALO:HALO + tm]
    zp1 = pltpu.roll(z, n - 1, 0)[HALO:HALO + tm]
    cw = cw_ref[...]
    zc = zm1 * cw[0:1] + z[HALO:HALO + tm] * cw[1:2] + zp1 * cw[2:3] + cb_ref[...]
    a = zc[:, :FF_TILE]
    g = zc[:, FF_TILE:]
    out_ref[...] = (g * _sigmoid(g) * a).astype(out_ref.dtype)


def _ffn_up(x, norm_w, w_up, conv_w, conv_b, tm):
    T, D = x.shape
    n_tiles = w_up.shape[1] // (2 * FF_TILE)
    per = tm // HALO
    last = T // HALO - 1
    return pl.pallas_call(
        functools.partial(_ffn_up_kernel, tm=tm),
        grid=(T // tm, n_tiles),
        in_specs=[
            pl.BlockSpec((tm, D), lambda i, j: (i, 0)),
            pl.BlockSpec((HALO, D), lambda i, j: (jnp.maximum(i * per - 1, 0), 0)),
            pl.BlockSpec((HALO, D), lambda i, j: (jnp.minimum((i + 1) * per, last), 0)),
            pl.BlockSpec((1, D), lambda i, j: (0, 0)),
            pl.BlockSpec((D, 2 * FF_TILE), lambda i, j: (0, j)),
            pl.BlockSpec((3, 2 * FF_TILE), lambda i, j: (0, j)),
            pl.BlockSpec((1, 2 * FF_TILE), lambda i, j: (0, j)),
        ],
        out_specs=pl.BlockSpec((tm, FF_TILE), lambda i, j: (i, j)),
        out_shape=jax.ShapeDtypeStruct((T, n_tiles * FF_TILE), BF16),
        scratch_shapes=[pltpu.VMEM((tm + 2 * HALO, D), BF16)],
        compiler_params=_params(("arbitrary", "arbitrary")),
        name="ffn_up",
    )(x, x, x, norm_w.reshape(1, D), w_up, conv_w, conv_b)


def _ffn_down_kernel(x_ref, a_ref, w_ref, out_ref):
    out_ref[...] = x_ref[...] + _dot(a_ref[...], w_ref[...])


def _ffn_down(x, act, w_down, tm, tn):
    T, D = x.shape
    K = act.shape[1]
    return pl.pallas_call(
        _ffn_down_kernel,
        grid=(T // tm, D // tn),
        in_specs=[
            pl.BlockSpec((tm, tn), lambda i, j: (i, j)),
            pl.BlockSpec((tm, K), lambda i, j: (i, 0)),
            pl.BlockSpec((K, tn), lambda i, j: (0, j)),
        ],
        out_specs=pl.BlockSpec((tm, tn), lambda i, j: (i, j)),
        out_shape=jax.ShapeDtypeStruct((T, D), F32),
        compiler_params=_params(("arbitrary", "arbitrary")),
        name="ffn_down",
    )(x, act, w_down)


def _final_norm_kernel(x_ref, w_ref, out_ref):
    xf = x_ref[...]
    ms = jnp.mean(xf * xf, axis=-1, keepdims=True)
    out_ref[...] = xf * lax.rsqrt(ms + EPS) * w_ref[...]


def _final_norm(x, w, tm):
    T, D = x.shape
    return pl.pallas_call(
        _final_norm_kernel,
        grid=(T // tm,),
        in_specs=[pl.BlockSpec((tm, D), lambda i: (i, 0)), pl.BlockSpec((1, D), lambda i: (0, 0))],
        out_specs=pl.BlockSpec((tm, D), lambda i: (i, 0)),
        out_shape=jax.ShapeDtypeStruct((T, D), F32),
        compiler_params=_params(("arbitrary",)),
        name="final_norm",
    )(x, w.reshape(1, D))


def _split_in_proj(w_in):
    a_end = 4 * W_GRP
    n_gate = 4 * N_HEADS
    w_main = jnp.concatenate([w_in[:, :a_end], w_in[:, a_end + n_gate:]], axis=1).astype(BF16)
    w_gate = jnp.pad(w_in[:, a_end:a_end + n_gate], ((0, 0), (0, GATE_PAD - n_gate))).astype(BF16)
    return w_main, w_gate


def _interleave_glu(a):
    lead = a.shape[:-1]
    pad = [(0, 0)] * len(lead) + [(0, D_FF_PAD - D_FF)]
    halves = [jnp.pad(h, pad).reshape(*lead, D_FF_PAD // FF_TILE, FF_TILE) for h in (a[..., :D_FF], a[..., D_FF:])]
    return jnp.stack(halves, axis=-2).reshape(*lead, 2 * D_FF_PAD)


def _tile(T, want):
    return min(want, T)


def kernel(x, norm1_w, w_in, mlstm_conv_w, mlstm_conv_b, mlstm_i_bias, mlstm_f_bias, mlstm_norm_w,
           hgrn_lb_logits, hgrn_f_bias, hgrn_norm_w, sgu_ln_w, sgu_ln_b, sgu_w, sgu_b, na_rel_bias,
           out_norm_w, w_out, norm2_w, ffn_w_up, ffn_conv_w, ffn_conv_b, ffn_w_down, final_norm_w):
    B, T, D = x.shape
    assert B == 1 and D == D_MODEL and T % (NA_ROWS * GRID_W) == 0
    depth = w_in.shape[0]
    tm = _tile(T, 1024)
    G = _tile(T, 512)

    lb_all = jax.nn.softmax(hgrn_lb_logits.astype(F32), axis=0)
    lower_bounds = jnp.cumsum(lb_all, axis=0) - lb_all[0]

    xs = x[0]
    for l in range(depth):
        w_main, w_gate = _split_in_proj(w_in[l])
        proj, gates = _proj_in(xs, norm1_w[l], w_main, w_gate, tm, 1024)
        y_a = _mlstm(proj, gates, mlstm_conv_w[l], mlstm_conv_b[l], mlstm_i_bias[l], mlstm_f_bias[l],
                     mlstm_norm_w[l], G)
        y_b = _hgrn(proj, lower_bounds[l], hgrn_f_bias[l], hgrn_norm_w[l], G)
        y_c = _sgu(proj, sgu_ln_w[l], sgu_ln_b[l], sgu_w[l], sgu_b[l], out_norm_w[l, :W_GRP], G)
        y_d = _na(proj, na_rel_bias[l], out_norm_w[l, W_GRP:], NA_ROWS)
        xs = _proj_out(xs, (y_a, y_b, y_c, y_d), w_out[l].astype(BF16), tm, 1024)
        act = _ffn_up(xs, norm2_w[l], _interleave_glu(ffn_w_up[l]).astype(BF16),
                      _interleave_glu(ffn_conv_w[l]), _interleave_glu(ffn_conv_b[l]).reshape(1, -1), _tile(T, 512))
        w_down = jnp.pad(ffn_w_down[l], ((0, D_FF_PAD - D_FF), (0, 0))).astype(BF16)
        xs = _ffn_down(xs, act, w_down, tm, 512)
    return _final_norm(xs, final_norm_w, tm)[None]
```

```python
import functools

import numpy as np
import jax
import jax.numpy as jnp
from jax import lax
from jax.experimental import pallas as pl
from jax.experimental.pallas import tpu as pltpu

F32 = jnp.float32
BF16 = jnp.bfloat16
HIGHEST = lax.Precision.HIGHEST

D_MODEL = 2048
HEAD_DIM = 128
N_HEADS = 4
W_GRP = N_HEADS * HEAD_DIM
D_FF = 5504
GRID_W = 64
NA_ROWS = 8
NA_COLS = 16
EPS = 1e-6
NEG_BIG = -1e30

LIN_CHUNK = 128
SGU_CHUNK = 128
GATE_PAD = 128
FF_TILE = 512
FF_SUB = 256
D_FF_PAD = 5632
HALO = 8

COL_A_Q, COL_A_K, COL_A_V, COL_A_O = 0, 1, 2, 3
COL_B_Q, COL_B_FF, COL_B_FB, COL_B_I, COL_B_G = 4, 5, 6, 7, 8
COL_C_U, COL_C_V = 9, 10
COL_D_Q, COL_D_K, COL_D_V = 11, 12, 13
N_MAIN = 14 * W_GRP

VMEM_LIMIT = 56 * 1024 * 1024

_NT = (((1,), (1,)), ((), ()))
_TN = (((0,), (0,)), ((), ()))


def _dot(a, b, dims=None, precision=None):
    if dims is None:
        return jnp.dot(a, b, preferred_element_type=F32, precision=precision)
    return lax.dot_general(a, b, dims, preferred_element_type=F32, precision=precision)


def _bdot(a, b, dims=None):
    return _dot(a.astype(BF16), b.astype(BF16), dims)


def _sigmoid(x):
    return jax.nn.sigmoid(x)


def _log_sigmoid(x):
    return jnp.minimum(x, 0.0) - jnp.log1p(jnp.exp(-jnp.abs(x)))


def _params(sem):
    return pltpu.CompilerParams(dimension_semantics=sem, vmem_limit_bytes=VMEM_LIMIT)


def _proj_in_kernel(x_ref, nw_ref, w_ref, wg_ref, proj_ref, gates_ref, h_ref):
    @pl.when(pl.program_id(1) == 0)
    def _():
        xf = x_ref[...]
        ms = jnp.mean(xf * xf, axis=-1, keepdims=True)
        h = (xf * lax.rsqrt(ms + EPS) * nw_ref[...]).astype(BF16)
        h_ref[...] = h
        gates_ref[...] = _dot(h, wg_ref[...])

    proj_ref[...] = _dot(h_ref[...], w_ref[...])


def _proj_in(x, norm_w, w_main, w_gate, l, tm, tn):
    T, D = x.shape
    N = w_main.shape[2]
    return pl.pallas_call(
        _proj_in_kernel,
        grid=(T // tm, N // tn),
        in_specs=[
            pl.BlockSpec((tm, D), lambda i, j: (i, 0)),
            pl.BlockSpec((1, D), lambda i, j: (0, 0)),
            pl.BlockSpec((None, D, tn), lambda i, j: (l, 0, j)),
            pl.BlockSpec((None, D, GATE_PAD), lambda i, j: (l, 0, 0)),
        ],
        out_specs=[
            pl.BlockSpec((tm, tn), lambda i, j: (i, j)),
            pl.BlockSpec((tm, GATE_PAD), lambda i, j: (i, 0)),
        ],
        out_shape=[
            jax.ShapeDtypeStruct((T, N), F32),
            jax.ShapeDtypeStruct((T, GATE_PAD), F32),
        ],
        scratch_shapes=[pltpu.VMEM((tm, D), BF16)],
        compiler_params=_params(("arbitrary", "arbitrary")),
        name="proj_in",
    )(x, norm_w.reshape(1, D), w_main, w_gate)


def _causal_mask(L, reverse):
    t = np.arange(L)[:, None]
    s = np.arange(L)[None, :]
    return ((s >= t) if reverse else (s <= t)).astype(np.float32)


def _shift_rows(x, prev_row, next_row):
    n = x.shape[0]
    rows = lax.broadcasted_iota(jnp.int32, (n, 1), 0)
    xm1 = jnp.where(rows == 0, prev_row, pltpu.roll(x, 1, 0))
    xp1 = jnp.where(rows == n - 1, next_row, pltpu.roll(x, n - 1, 0))
    return xm1, xp1


def _dwconv3(x, prev_row, next_row, w3, b):
    xm1, xp1 = _shift_rows(x, prev_row, next_row)
    return xm1 * w3[0:1] + x * w3[1:2] + xp1 * w3[2:3] + b


def _halo_specs(G, T, width, col, rev):
    nb = T // G
    per = G // HALO
    last = T // HALO - 1

    def blk(c):
        return (nb - 1 - c) if rev else c

    prev = pl.BlockSpec((HALO, width), lambda c: (jnp.maximum(blk(c) * per - 1, 0), col))
    nxt = pl.BlockSpec((HALO, width), lambda c: (jnp.minimum((blk(c) + 1) * per, last), col))
    return prev, nxt


def _mlstm_kernel(*refs, reverse, G, L):
    if reverse:
        (q_ref, qp_ref, qn_ref, k_ref, kp_ref, kn_ref, v_ref, g_ref, gt_ref, gb_ref, gbt_ref,
         cw_ref, cb_ref, mask_ref, hf_ref, o_ref, nw_ref, out_ref,
         qc_ref, kc_ref, ct_ref, m_ref) = refs
    else:
        (q_ref, qp_ref, qn_ref, k_ref, kp_ref, kn_ref, v_ref, g_ref, gt_ref, gb_ref, gbt_ref,
         cw_ref, cb_ref, mask_ref, out_ref,
         qc_ref, kc_ref, ct_ref, m_ref) = refs
    c = pl.program_id(0)
    nblk = pl.num_programs(0)
    blk = (nblk - 1 - c) if reverse else c

    @pl.when(c == 0)
    def _():
        ct_ref[...] = jnp.zeros_like(ct_ref)
        m_ref[...] = jnp.zeros_like(m_ref)

    has_prev = (blk > 0).astype(F32)
    has_next = (blk < nblk - 1).astype(F32)
    cw = cw_ref[...]
    cb = cb_ref[...]
    zq = _dwconv3(q_ref[...], qp_ref[HALO - 1:HALO, :] * has_prev, qn_ref[0:1, :] * has_next,
                  cw[:, :W_GRP], cb[:, :W_GRP])
    qc_ref[...] = zq * _sigmoid(zq)
    zk = _dwconv3(k_ref[...], kp_ref[HALO - 1:HALO, :] * has_prev, kn_ref[0:1, :] * has_next,
                  cw[:, W_GRP:], cb[:, W_GRP:])
    kc_ref[...] = zk * _sigmoid(zk) * (HEAD_DIM ** -0.5)

    maskf = mask_ref[...]
    mask = maskf > 0.5
    ones_col = (lax.broadcasted_iota(jnp.int32, (L, HEAD_DIM), 1) == 0).astype(F32)
    d = 1 if reverse else 0
    nsub = G // L

    def chunk(j, carry):
        jj = (nsub - 1 - j) if reverse else j
        r0 = pl.multiple_of(jj * L, L)
        rows = pl.ds(r0, L)
        gch = g_ref[rows, :] + gb_ref[...]
        b_cols = _dot(maskf, _log_sigmoid(gch), precision=HIGHEST)
        gt = gt_ref[jj] + gbt_ref[...]
        b_rows = _dot(_log_sigmoid(gt), maskf, _NT, precision=HIGHEST)
        heads = range(N_HEADS)
        hcs = [slice(h * HEAD_DIM, (h + 1) * HEAD_DIM) for h in heads]
        qh = [qc_ref[rows, hc].astype(BF16) for hc in hcs]
        kh = [kc_ref[rows, hc].astype(BF16) for hc in hcs]
        v_aug = [jnp.concatenate([v_ref[rows, hc], ones_col], axis=1) for hc in hcs]
        ct = [ct_ref[h] for h in heads]
        m_prev = [m_ref[h][0:1, 0:1] for h in heads]
        qk = [_dot(qh[h], kh[h], _NT) for h in heads]
        qc = [_dot(qh[h], ct[h].astype(BF16)) for h in heads]
        w_intra, w_inter, m_t, decay, upd = [], [], [], [], []
        for h in heads:
            ci = d * N_HEADS + h
            cf = 2 * N_HEADS + ci
            i_col, b_col = gch[:, ci:ci + 1], b_cols[:, cf:cf + 1]
            i_row, b_row = gt[ci:ci + 1, :], b_rows[cf:cf + 1, :]
            dmat = jnp.where(mask, b_col - b_row + i_row, NEG_BIG)
            inter = b_col + m_prev[h]
            m_t.append(jnp.maximum(inter, jnp.max(dmat, axis=1, keepdims=True)))
            w_intra.append(jnp.where(mask, jnp.exp(dmat - m_t[h]), 0.0))
            w_inter.append(jnp.exp(inter - m_t[h]))
            b_last = b_col[0:1] if reverse else b_col[L - 1:L]
            a_col = i_col + b_last - b_col
            m_new = jnp.maximum(b_last + m_prev[h], jnp.max(a_col, axis=0, keepdims=True))
            decay.append(jnp.exp(b_last + m_prev[h] - m_new))
            wk = jnp.exp(a_col - m_new)
            upd.append(_bdot(kh[h], wk * v_aug[h], _TN))
            m_ref[h] = jnp.broadcast_to(m_new, m_ref.shape[1:])
        sv = [_bdot(qk[h] * w_intra[h], v_aug[h]) for h in heads]
        for h in heads:
            hc = hcs[h]
            ct_ref[h] = decay[h] * ct[h] + upd[h]
            num = sv[h] + w_inter[h] * qc[h]
            den = num[:, HEAD_DIM:HEAD_DIM + 1]
            hh = num[:, :HEAD_DIM] / jnp.maximum(jnp.abs(den), jnp.exp(-m_t[h]))
            if reverse:
                tot = hf_ref[rows, hc] + hh
                mu = jnp.mean(tot, axis=-1, keepdims=True)
                tc = tot - mu
                var = jnp.mean(tc * tc, axis=-1, keepdims=True)
                y = tc * lax.rsqrt(var + EPS) * nw_ref[:, hc] * _sigmoid(o_ref[rows, hc])
                out_ref[rows, hc] = y.astype(out_ref.dtype)
            else:
                out_ref[rows, hc] = hh
        return carry

    lax.fori_loop(0, nsub, chunk, 0, unroll=2)


def _mlstm_dir(proj, gates, gates_t, gb, gbt, conv_w, conv_b, reverse, G, hf=None, norm_w=None):
    T = proj.shape[0]
    L = LIN_CHUNK
    nb = T // G

    def blk(c):
        return (nb - 1 - c) if reverse else c

    def col_spec(col):
        return pl.BlockSpec((G, W_GRP), lambda c: (blk(c), col))

    qp, qn = _halo_specs(G, T, W_GRP, COL_A_Q, reverse)
    kp, kn = _halo_specs(G, T, W_GRP, COL_A_K, reverse)
    const2 = lambda shape: pl.BlockSpec(shape, lambda c: (0, 0))
    in_specs = [
        col_spec(COL_A_Q), qp, qn, col_spec(COL_A_K), kp, kn, col_spec(COL_A_V),
        pl.BlockSpec((G, GATE_PAD), lambda c: (blk(c), 0)),
        pl.BlockSpec((G // L, 4 * N_HEADS, L), lambda c: (blk(c), 0, 0)),
        const2((1, GATE_PAD)), const2((4 * N_HEADS, 1)),
        const2((3, 2 * W_GRP)), const2((1, 2 * W_GRP)), const2((L, L)),
    ]
    args = [proj, proj, proj, proj, proj, proj, proj, gates, gates_t, gb, gbt,
            conv_w, conv_b.reshape(1, -1), jnp.asarray(_causal_mask(L, reverse))]
    if reverse:
        in_specs += [pl.BlockSpec((G, W_GRP), lambda c: (blk(c), 0)), col_spec(COL_A_O),
                     const2((1, W_GRP))]
        args += [hf, proj, norm_w.reshape(1, -1)]
    return pl.pallas_call(
        functools.partial(_mlstm_kernel, reverse=reverse, G=G, L=L),
        grid=(nb,),
        in_specs=in_specs,
        out_specs=pl.BlockSpec((G, W_GRP), lambda c: (blk(c), 0)),
        out_shape=jax.ShapeDtypeStruct((T, W_GRP), BF16 if reverse else F32),
        scratch_shapes=[
            pltpu.VMEM((G, W_GRP), F32), pltpu.VMEM((G, W_GRP), F32),
            pltpu.VMEM((N_HEADS, HEAD_DIM, 2 * HEAD_DIM), F32),
            pltpu.VMEM((N_HEADS, 8, 128), F32),
        ],
        compiler_params=_params(("arbitrary",)),
        name="mlstm_bwd" if reverse else "mlstm_fwd",
    )(*args)


def _mlstm(proj, gates, conv_w, conv_b, i_bias, f_bias, norm_w, G):
    T = proj.shape[0]
    L = LIN_CHUNK
    ng = 4 * N_HEADS
    gbias = jnp.concatenate([i_bias.reshape(-1), f_bias.reshape(-1)]).astype(F32)
    gb = jnp.pad(gbias, (0, GATE_PAD - ng)).reshape(1, GATE_PAD)
    gbt = gbias.reshape(ng, 1)
    gates_t = gates[:, :ng].reshape(T // L, L, ng).transpose(0, 2, 1)
    hf = _mlstm_dir(proj, gates, gates_t, gb, gbt, conv_w, conv_b, False, G)
    return _mlstm_dir(proj, gates, gates_t, gb, gbt, conv_w, conv_b, True, G, hf, norm_w)


def _hgrn_level_masks(L, reverse):
    n_lev = int(np.log2(L))
    t = np.arange(L)
    lmask = []
    for lev in range(n_lev):
        h = 1 << lev
        blk = t // (2 * h)
        upper = (t // h) % 2 == 1
        is_q = ~upper if reverse else upper
        same = blk[:, None] == blk[None, :]
        lmask.append((same & is_q[:, None] & (~is_q)[None, :]).astype(np.float32))
    lmask.append(np.eye(L, dtype=np.float32))
    return np.stack(lmask)


def _anchor_rows(b, lev, reverse):
    L, W = b.shape
    h = 1 << lev
    a = h if reverse else h - 1
    if 2 * h >= 8:
        b3 = b.reshape(L // (2 * h), 2 * h, W)
        return jnp.broadcast_to(b3[:, a:a + 1, :], b3.shape).reshape(L, W)
    b3 = b.reshape(L // 8, 8, W)
    sub = lax.broadcasted_iota(jnp.int32, (1, 8, 1), 1)
    out = None
    for g in range(8 // (2 * h)):
        piece = jnp.broadcast_to(b3[:, g * 2 * h + a:g * 2 * h + a + 1, :], b3.shape)
        out = piece if out is None else jnp.where(sub >= g * 2 * h, piece, out)
    return out.reshape(L, W)


def _hgrn_kernel(*refs, reverse, G, L):
    if reverse:
        (q_ref, f_ref, i_ref, lb_ref, fb_ref, mask_ref, lmask_ref, of_ref, g_ref, nw_ref,
         out_ref, st_ref) = refs
    else:
        (q_ref, f_ref, i_ref, lb_ref, fb_ref, mask_ref, lmask_ref, out_ref, st_ref) = refs
    c = pl.program_id(0)

    @pl.when(c == 0)
    def _():
        st_ref[...] = jnp.zeros_like(st_ref)

    n_lev = lmask_ref.shape[0] - 1
    nsub = G // L
    heads = range(N_HEADS)
    hcs = [slice(h * HEAD_DIM, (h + 1) * HEAD_DIM) for h in heads]
    t_idx = lax.broadcasted_iota(jnp.int32, (L, 1), 0)

    def chunk(j, carry):
        jj = (nsub - 1 - j) if reverse else j
        r0 = pl.multiple_of(jj * L, L)
        rows = pl.ds(r0, L)
        lb = lb_ref[...]
        z = f_ref[rows, :] + fb_ref[...]
        logf = jnp.log(lb + (1.0 - lb) * _sigmoid(z))
        key = (1.0 - lb) * _sigmoid(-z)
        qs = q_ref[rows, :]
        qs = qs * _sigmoid(qs)
        b = _dot(mask_ref[...], logf, precision=HIGHEST)
        b_last = b[0:1] if reverse else b[L - 1:L]
        vh = [i_ref[rows, hc].astype(BF16) for hc in hcs]
        st = [st_ref[h] for h in heads]
        q_in = (qs * jnp.exp(b)).astype(BF16)
        k_out = (key * jnp.exp(b_last - b)).astype(BF16)
        inter = [_dot(q_in[:, hcs[h]], st[h].astype(BF16), _NT) for h in heads]
        upd = [_dot(vh[h], k_out[:, hcs[h]], _TN) for h in heads]
        qb, kb = qs.astype(BF16), key.astype(BF16)
        scores = [lmask_ref[n_lev] * _dot(qb[:, hcs[h]], kb[:, hcs[h]], _NT) for h in heads]
        for lev in range(n_lev):
            is_q = ((t_idx >> lev) & 1) == (0 if reverse else 1)
            x = (jnp.where(is_q, qs, key) * jnp.exp(-jnp.abs(b - _anchor_rows(b, lev, reverse)))).astype(BF16)
            for h in heads:
                scores[h] = scores[h] + lmask_ref[lev] * _dot(x[:, hcs[h]], x[:, hcs[h]], _NT)
        decay = jnp.exp(b_last)
        for h in heads:
            hc = hcs[h]
            st_ref[h] = st[h] * decay[:, hc] + upd[h]
            o = _dot(scores[h].astype(BF16), vh[h]) + inter[h]
            if reverse:
                tot = of_ref[rows, hc] + o
                ms = jnp.mean(tot * tot, axis=-1, keepdims=True)
                gg = g_ref[rows, hc]
                y = tot * lax.rsqrt(ms + EPS) * nw_ref[:, hc] * (gg * _sigmoid(gg))
                out_ref[rows, hc] = y.astype(out_ref.dtype)
            else:
                out_ref[rows, hc] = o
        return carry

    lax.fori_loop(0, nsub, chunk, 0, unroll=2)


def _hgrn_dir(proj, lb, fb, reverse, G, of=None, norm_w=None):
    T = proj.shape[0]
    L = LIN_CHUNK
    nb = T // G
    mask = _causal_mask(L, reverse)
    lmask = _hgrn_level_masks(L, reverse)

    def blk(c):
        return (nb - 1 - c) if reverse else c

    def col_spec(col):
        return pl.BlockSpec((G, W_GRP), lambda c: (blk(c), col))

    const2 = lambda shape: pl.BlockSpec(shape, lambda c: (0, 0))
    const3 = lambda shape: pl.BlockSpec(shape, lambda c: (0, 0, 0))
    in_specs = [col_spec(COL_B_Q), col_spec(COL_B_FB if reverse else COL_B_FF), col_spec(COL_B_I),
                const2((1, W_GRP)), const2((1, W_GRP)), const2(mask.shape), const3(lmask.shape)]
    args = [proj, proj, proj, lb.reshape(1, -1), fb.reshape(1, -1), jnp.asarray(mask), jnp.asarray(lmask)]
    if reverse:
        in_specs += [pl.BlockSpec((G, W_GRP), lambda c: (blk(c), 0)), col_spec(COL_B_G), const2((1, W_GRP))]
        args += [of, proj, norm_w.reshape(1, -1)]
    return pl.pallas_call(
        functools.partial(_hgrn_kernel, reverse=reverse, G=G, L=L),
        grid=(nb,),
        in_specs=in_specs,
        out_specs=pl.BlockSpec((G, W_GRP), lambda c: (blk(c), 0)),
        out_shape=jax.ShapeDtypeStruct((T, W_GRP), BF16 if reverse else F32),
        scratch_shapes=[pltpu.VMEM((N_HEADS, HEAD_DIM, HEAD_DIM), F32)],
        compiler_params=_params(("arbitrary",)),
        name="hgrn_bwd" if reverse else "hgrn_fwd",
    )(*args)


def _hgrn(proj, lb, f_bias, norm_w, G):
    of = _hgrn_dir(proj, lb, f_bias[0], False, G)
    return _hgrn_dir(proj, lb, f_bias[1], True, G, of, norm_w)


def _gelu(x):
    return 0.5 * x * (1.0 + lax.erf(x * (2.0 ** -0.5)))


def _sgu_kernel(u_ref, v_ref, lnw_ref, lnb_ref, ws_ref, bs_ref, nw_ref, out_ref, *, G):
    v = _gelu(v_ref[...])
    mu = jnp.mean(v, axis=-1, keepdims=True)
    vc = v - mu
    var = jnp.mean(vc * vc, axis=-1, keepdims=True)
    vn = (vc * lax.rsqrt(var + EPS) * lnw_ref[...] + lnb_ref[...]).astype(BF16)
    for n in range(G // SGU_CHUNK):
        rows = slice(n * SGU_CHUNK, (n + 1) * SGU_CHUNK)
        for g in range(N_HEADS):
            hc = slice(g * HEAD_DIM, (g + 1) * HEAD_DIM)
            mixed = _dot(ws_ref[g], vn[rows, hc]) + bs_ref[:, g:g + 1]
            y = _gelu(u_ref[rows, hc]) * mixed
            ms = jnp.mean(y * y, axis=-1, keepdims=True)
            out_ref[rows, hc] = (y * lax.rsqrt(ms + EPS) * nw_ref[:, hc]).astype(out_ref.dtype)


def _sgu(proj, ln_w, ln_b, w_s, b_s, norm_w, G):
    T = proj.shape[0]
    row = lambda a: a.reshape(1, -1)
    const2 = lambda shape: pl.BlockSpec(shape, lambda c: (0, 0))
    return pl.pallas_call(
        functools.partial(_sgu_kernel, G=G),
        grid=(T // G,),
        in_specs=[
            pl.BlockSpec((G, W_GRP), lambda c: (c, COL_C_U)),
            pl.BlockSpec((G, W_GRP), lambda c: (c, COL_C_V)),
            const2((1, W_GRP)), const2((1, W_GRP)),
            pl.BlockSpec((N_HEADS, SGU_CHUNK, SGU_CHUNK), lambda c: (0, 0, 0)),
            const2((SGU_CHUNK, N_HEADS)), const2((1, W_GRP)),
        ],
        out_specs=pl.BlockSpec((G, W_GRP), lambda c: (c, 0)),
        out_shape=jax.ShapeDtypeStruct((T, W_GRP), BF16),
        compiler_params=_params(("arbitrary",)),
        name="sgu",
    )(proj, proj, row(ln_w), row(ln_b), w_s.astype(BF16), b_s.T, row(norm_w))


def _na_bias_bands(rel_bias):
    cols = np.arange(GRID_W)
    col_start = np.clip(cols - NA_COLS // 2, 0, GRID_W - NA_COLS)
    cc = cols[None, :]
    valid = (cc >= col_start[:, None]) & (cc < col_start[:, None] + NA_COLS)
    col_off = cc - cols[:, None] + (NA_COLS - 1)
    n_off = 2 * NA_COLS - 1
    onehot = ((col_off[None] == np.arange(n_off)[:, None, None]) & valid[None]).astype(np.float32)
    lead = rel_bias.shape[:-1]
    bands = jnp.dot(rel_bias.astype(F32).reshape(-1, n_off), jnp.asarray(onehot.reshape(n_off, -1)),
                    precision=HIGHEST)
    bands = bands + jnp.asarray(np.where(valid, 0.0, NEG_BIG).astype(np.float32).reshape(1, -1))
    return bands.reshape(*lead, GRID_W, GRID_W)


def _na_kernel(q_ref, k_ref, v_ref, bands_ref, nw_ref, out_ref, bias_ref, *, RB, n_rows):
    R = pl.program_id(1)
    win = NA_ROWS * GRID_W
    scale = HEAD_DIM ** -0.5

    @pl.when(R == 0)
    def _():
        for e in range(NA_ROWS):
            for j in range(NA_ROWS):
                bias_ref[e, :, j * GRID_W:(j + 1) * GRID_W] = bands_ref[j - e + NA_ROWS - 1]

    krows, es = [], []
    for i in range(RB):
        r = R * RB + i
        rs = jnp.clip(r - NA_ROWS // 2, 0, n_rows - NA_ROWS)
        es.append(r - rs)
        krows.append(pl.ds(pl.multiple_of(rs * GRID_W, GRID_W), win))
    qrows = [slice(i * GRID_W, (i + 1) * GRID_W) for i in range(RB)]
    ss = [_bdot(q_ref[qrows[i], :], k_ref[krows[i], :], _NT) for i in range(RB)]
    ps, sums = [], []
    for i in range(RB):
        s = ss[i] * scale + bias_ref[es[i]]
        p = jnp.exp(s - jnp.max(s, axis=-1, keepdims=True))
        sums.append(jnp.sum(p, axis=-1, keepdims=True))
        ps.append(p.astype(BF16))
    os_ = [_dot(ps[i], v_ref[krows[i], :].astype(BF16)) for i in range(RB)]
    for i in range(RB):
        o = os_[i] / sums[i]
        ms = jnp.mean(o * o, axis=-1, keepdims=True)
        out_ref[qrows[i], :] = (o * lax.rsqrt(ms + EPS) * nw_ref[...]).astype(out_ref.dtype)


def _na(proj, bands, norm_w, l, RB):
    T = proj.shape[0]
    n_rows = T // GRID_W
    cq, ck, cv = (c * N_HEADS for c in (COL_D_Q, COL_D_K, COL_D_V))
    return pl.pallas_call(
        functools.partial(_na_kernel, RB=RB, n_rows=n_rows),
        grid=(N_HEADS, n_rows // RB),
        in_specs=[
            pl.BlockSpec((RB * GRID_W, HEAD_DIM), lambda h, r: (r, cq + h)),
            pl.BlockSpec((T, HEAD_DIM), lambda h, r: (0, ck + h)),
            pl.BlockSpec((T, HEAD_DIM), lambda h, r: (0, cv + h)),
            pl.BlockSpec((None, None, 2 * NA_ROWS - 1, GRID_W, GRID_W), lambda h, r: (l, h, 0, 0, 0)),
            pl.BlockSpec((1, HEAD_DIM), lambda h, r: (0, h)),
        ],
        out_specs=pl.BlockSpec((RB * GRID_W, HEAD_DIM), lambda h, r: (r, h)),
        out_shape=jax.ShapeDtypeStruct((T, W_GRP), BF16),
        scratch_shapes=[pltpu.VMEM((NA_ROWS, GRID_W, NA_ROWS * GRID_W), F32)],
        compiler_params=_params(("arbitrary", "arbitrary")),
        name="na",
    )(proj, proj, proj, bands, norm_w.reshape(1, -1))


def _proj_out_kernel(x_ref, ya_ref, yb_ref, yc_ref, yd_ref, wa_ref, wb_ref, wc_ref, wd_ref, out_ref):
    acc = x_ref[...]
    for y_ref, w_ref in ((ya_ref, wa_ref), (yb_ref, wb_ref), (yc_ref, wc_ref), (yd_ref, wd_ref)):
        acc = acc + _dot(y_ref[...], w_ref[...])
    out_ref[...] = acc


def _proj_out(x, ys, w_out, l, tm, tn):
    T, D = x.shape
    y_spec = pl.BlockSpec((tm, W_GRP), lambda i, j: (i, 0))
    w_specs = [pl.BlockSpec((None, W_GRP, tn), functools.partial(lambda i, j, g: (l, g, j), g=g)) for g in range(4)]
    return pl.pallas_call(
        _proj_out_kernel,
        grid=(T // tm, D // tn),
        in_specs=[pl.BlockSpec((tm, tn), lambda i, j: (i, j))] + [y_spec] * 4 + w_specs,
        out_specs=pl.BlockSpec((tm, tn), lambda i, j: (i, j)),
        out_shape=jax.ShapeDtypeStruct((T, D), F32),
        compiler_params=_params(("arbitrary", "arbitrary")),
        name="proj_out",
    )(x, *ys, w_out, w_out, w_out, w_out)


def _ffn_up_kernel(x_ref, xp_ref, xn_ref, nw_ref, wa_ref, wg_ref, cwa_ref, cwg_ref, cba_ref, cbg_ref,
                   out_ref, h_ref, z_ref, *, tm):
    i = pl.program_id(0)
    nblk = pl.num_programs(0)
    ng = tm // HALO
    sub = lax.broadcasted_iota(jnp.int32, (1, HALO, 1), 1)

    def conv(z, cw, cb):
        n = z.shape[1]
        z3 = z.reshape(ng + 2, HALO, n)
        down = pltpu.roll(z3, 1, 1)
        up = pltpu.roll(z3, HALO - 1, 1)
        zm1 = jnp.where(sub == 0, down[0:ng], down[1:ng + 1])
        zp1 = jnp.where(sub == HALO - 1, up[2:ng + 2], up[1:ng + 1])
        out = zm1 * cw[0:1] + z3[1:ng + 1] * cw[1:2] + zp1 * cw[2:3] + cb
        return out.reshape(tm, n)

    def norm(xf):
        ms = jnp.mean(xf * xf, axis=-1, keepdims=True)
        return xf * lax.rsqrt(ms + EPS) * nw_ref[...]

    @pl.when(pl.program_id(1) == 0)
    def _():
        h_ref[0:HALO, :] = (norm(xp_ref[...]) * (i > 0).astype(F32)).astype(BF16)
        h_ref[HALO:HALO + tm, :] = norm(x_ref[...]).astype(BF16)
        h_ref[HALO + tm:, :] = (norm(xn_ref[...]) * (i < nblk - 1).astype(F32)).astype(BF16)

    h = h_ref[...]
    slot = lax.rem(pl.program_id(1), 2)
    for s in range(FF_TILE // FF_SUB):
        cs = slice(s * FF_SUB, (s + 1) * FF_SUB)
        z_ref[slot, s, 0] = _dot(h, wa_ref[:, cs])
        z_ref[slot, s, 1] = _dot(h, wg_ref[:, cs])
    for s in range(FF_TILE // FF_SUB):
        cs = slice(s * FF_SUB, (s + 1) * FF_SUB)
        a = conv(z_ref[slot, s, 0], cwa_ref[:, cs], cba_ref[:, cs])
        g = conv(z_ref[slot, s, 1], cwg_ref[:, cs], cbg_ref[:, cs])
        out_ref[:, cs] = (g * _sigmoid(g) * a).astype(out_ref.dtype)


def _ffn_up(x, norm_w, w_up, conv_w, conv_b, l, tm):
    T, D = x.shape
    n_tiles = D_FF_PAD // FF_TILE
    per = tm // HALO
    last = T // HALO - 1

    def halves(rows):
        return [pl.BlockSpec((None, rows, FF_TILE), lambda i, j: (l, 0, j)),
                pl.BlockSpec((None, rows, FF_TILE), lambda i, j: (l, 0, n_tiles + j))]

    return pl.pallas_call(
        functools.partial(_ffn_up_kernel, tm=tm),
        grid=(T // tm, n_tiles),
        in_specs=[
            pl.BlockSpec((tm, D), lambda i, j: (i, 0)),
            pl.BlockSpec((HALO, D), lambda i, j: (jnp.maximum(i * per - 1, 0), 0)),
            pl.BlockSpec((HALO, D), lambda i, j: (jnp.minimum((i + 1) * per, last), 0)),
            pl.BlockSpec((1, D), lambda i, j: (0, 0)),
        ] + halves(D) + halves(3) + halves(1),
        out_specs=pl.BlockSpec((tm, FF_TILE), lambda i, j: (i, j)),
        out_shape=jax.ShapeDtypeStruct((T, D_FF_PAD), BF16),
        scratch_shapes=[pltpu.VMEM((tm + 2 * HALO, D), BF16),
                        pltpu.VMEM((2, FF_TILE // FF_SUB, 2, tm + 2 * HALO, FF_SUB), F32)],
        compiler_params=_params(("arbitrary", "arbitrary")),
        name="ffn_up",
    )(x, x, x, norm_w.reshape(1, D), w_up, w_up, conv_w, conv_w, conv_b, conv_b)


def _ffn_down_kernel(x_ref, a_ref, w_ref, out_ref):
    out_ref[...] = x_ref[...] + _dot(a_ref[...], w_ref[...])


def _ffn_down(x, act, w_down, l, tm, tn):
    T, D = x.shape
    K = act.shape[1]
    return pl.pallas_call(
        _ffn_down_kernel,
        grid=(T // tm, D // tn),
        in_specs=[
            pl.BlockSpec((tm, tn), lambda i, j: (i, j)),
            pl.BlockSpec((tm, K), lambda i, j: (i, 0)),
            pl.BlockSpec((None, K, tn), lambda i, j: (l, 0, j)),
        ],
        out_specs=pl.BlockSpec((tm, tn), lambda i, j: (i, j)),
        out_shape=jax.ShapeDtypeStruct((T, D), F32),
        compiler_params=_params(("arbitrary", "arbitrary")),
        name="ffn_down",
    )(x, act, w_down)


def _final_norm_kernel(x_ref, w_ref, out_ref):
    xf = x_ref[...]
    ms = jnp.mean(xf * xf, axis=-1, keepdims=True)
    out_ref[...] = xf * lax.rsqrt(ms + EPS) * w_ref[...]


def _final_norm(x, w, tm):
    T, D = x.shape
    return pl.pallas_call(
        _final_norm_kernel,
        grid=(T // tm,),
        in_specs=[pl.BlockSpec((tm, D), lambda i: (i, 0)), pl.BlockSpec((1, D), lambda i: (0, 0))],
        out_specs=pl.BlockSpec((tm, D), lambda i: (i, 0)),
        out_shape=jax.ShapeDtypeStruct((T, D), F32),
        compiler_params=_params(("arbitrary",)),
        name="final_norm",
    )(x, w.reshape(1, D))


def _split_in_proj(w_in):
    a_end = 4 * W_GRP
    n_gate = 4 * N_HEADS
    w = w_in.astype(BF16)
    w_main = jnp.concatenate([w[..., :a_end], w[..., a_end + n_gate:]], axis=-1)
    zeros = jnp.zeros(w.shape[:-1] + (GATE_PAD - n_gate,), BF16)
    w_gate = jnp.concatenate([w[..., a_end:a_end + n_gate], zeros], axis=-1)
    return w_main, w_gate


def _pad_glu(a):
    zeros = jnp.zeros(a.shape[:-1] + (D_FF_PAD - D_FF,), a.dtype)
    return jnp.concatenate([a[..., :D_FF], zeros, a[..., D_FF:], zeros], axis=-1)


def _tile(T, want):
    return min(want, T)


def kernel(x, norm1_w, w_in, mlstm_conv_w, mlstm_conv_b, mlstm_i_bias, mlstm_f_bias, mlstm_norm_w,
           hgrn_lb_logits, hgrn_f_bias, hgrn_norm_w, sgu_ln_w, sgu_ln_b, sgu_w, sgu_b, na_rel_bias,
           out_norm_w, w_out, norm2_w, ffn_w_up, ffn_conv_w, ffn_conv_b, ffn_w_down, final_norm_w):
    B, T, D = x.shape
    assert B == 1 and D == D_MODEL and T % (NA_ROWS * GRID_W) == 0
    depth = w_in.shape[0]
    tm = _tile(T, 1024)
    G = _tile(T, 512)

    lb_all = jax.nn.softmax(hgrn_lb_logits.astype(F32), axis=0)
    lower_bounds = jnp.cumsum(lb_all, axis=0) - lb_all[0]

    w_main, w_gate = _split_in_proj(w_in)
    w_out_b = w_out.astype(BF16)
    w_up = _pad_glu(ffn_w_up.astype(BF16))
    conv_w = _pad_glu(ffn_conv_w.astype(F32))
    conv_b = _pad_glu(ffn_conv_b.astype(F32))[:, None, :]
    w_dn = ffn_w_down.astype(BF16)
    w_dn = jnp.concatenate([w_dn, jnp.zeros((depth, D_FF_PAD - D_FF, D), BF16)], axis=1)
    bands = _na_bias_bands(na_rel_bias)

    xs = x[0]
    for l in range(depth):
        proj, gates = _proj_in(xs, norm1_w[l], w_main, w_gate, l, tm, 1024)
        y_a = _mlstm(proj, gates, mlstm_conv_w[l], mlstm_conv_b[l], mlstm_i_bias[l], mlstm_f_bias[l],
                     mlstm_norm_w[l], G)
        y_b = _hgrn(proj, lower_bounds[l], hgrn_f_bias[l], hgrn_norm_w[l], G)
        y_c = _sgu(proj, sgu_ln_w[l], sgu_ln_b[l], sgu_w[l], sgu_b[l], out_norm_w[l, :W_GRP], G)
        y_d = _na(proj, bands, out_norm_w[l, W_GRP:], l, NA_ROWS)
        xs = _proj_out(xs, (y_a, y_b, y_c, y_d), w_out_b, l, tm, 1024)
        act = _ffn_up(xs, norm2_w[l], w_up, conv_w, conv_b, l, _tile(T, 512))
        xs = _ffn_down(xs, act, w_dn, l, tm, 512)
    return _final_norm(xs, final_norm_w, tm)[None]
```

```python
import functools

import numpy as np
import jax
import jax.numpy as jnp
from jax import lax
from jax.experimental import pallas as pl
from jax.experimental.pallas import tpu as pltpu

F32 = jnp.float32
BF16 = jnp.bfloat16
HIGHEST = lax.Precision.HIGHEST

D_MODEL = 2048
HEAD_DIM = 128
N_HEADS = 4
W_GRP = N_HEADS * HEAD_DIM
D_FF = 5504
GRID_W = 64
NA_ROWS = 8
NA_COLS = 16
EPS = 1e-6
NEG_BIG = -1e30

LIN_CHUNK = 128
SGU_CHUNK = 128
GATE_PAD = 128
FF_TILE = 512
FF_SUB = 256
HALO = 8
LANE = 128

COL_A_Q, COL_A_K, COL_A_O = 0, 1, 2
COL_B_Q, COL_B_FF, COL_B_FB, COL_B_G = 3, 4, 5, 6
COL_C_U, COL_C_V = 7, 8
N_F32 = 9
COL_A_V, COL_B_I, COL_D_Q, COL_D_K, COL_D_V = 0, 1, 2, 3, 4
N_BF16 = 5
IN_PROJ_SLICES = ((0, 1024), (1536, 2048), (2064, 3600), (4112, 4624), (4624, 5648),
                  (1024, 1536), (3600, 4112), (5648, 7184))
GATE_SLICE = (2048, 2064)

VMEM_LIMIT = 56 * 1024 * 1024

_NT = (((1,), (1,)), ((), ()))
_TN = (((0,), (0,)), ((), ()))


def _dot(a, b, dims=None, precision=None):
    if dims is None:
        return jnp.dot(a, b, preferred_element_type=F32, precision=precision)
    return lax.dot_general(a, b, dims, preferred_element_type=F32, precision=precision)


def _bdot(a, b, dims=None):
    return _dot(a.astype(BF16), b.astype(BF16), dims)


def _sigmoid(x):
    return jax.nn.sigmoid(x)


def _log_sigmoid(x):
    return jnp.minimum(x, 0.0) - jnp.log1p(jnp.exp(-jnp.abs(x)))


def _neg_abs(x):
    bits = pltpu.bitcast(x, jnp.int32) | jnp.int32(-2 ** 31)
    return pltpu.bitcast(bits, F32)


def _head_mean(x):
    hi = x.astype(BF16)
    lo = (x - hi.astype(F32)).astype(BF16)
    avg = jnp.full((HEAD_DIM, HEAD_DIM), 1.0 / HEAD_DIM, BF16)
    return _dot(hi, avg) + _dot(lo, avg)


def _params(sem):
    return pltpu.CompilerParams(dimension_semantics=sem, vmem_limit_bytes=VMEM_LIMIT)


def _proj_in_kernel(x_ref, nw_ref, w_ref, wg_ref, pf_ref, pb_ref, gates_ref, h_ref):
    j = pl.program_id(1)

    @pl.when(j == 0)
    def _():
        xf = x_ref[...]
        ms = jnp.mean(xf * xf, axis=-1, keepdims=True)
        h = (xf * lax.rsqrt(ms + EPS) * nw_ref[...]).astype(BF16)
        h_ref[...] = h
        gates_ref[...] = _dot(h, wg_ref[...])

    acc = _dot(h_ref[...], w_ref[...])

    @pl.when(j < N_F32)
    def _():
        pf_ref[...] = acc

    @pl.when(j >= N_F32)
    def _():
        pb_ref[...] = acc.astype(BF16)


def _proj_in(x, norm_w, w_main, w_gate, l, tm):
    T, D = x.shape
    return pl.pallas_call(
        _proj_in_kernel,
        grid=(T // tm, N_F32 + N_BF16),
        in_specs=[
            pl.BlockSpec((tm, D), lambda i, j: (i, 0)),
            pl.BlockSpec((1, D), lambda i, j: (0, 0)),
            pl.BlockSpec((None, D, W_GRP), lambda i, j: (l, 0, j)),
            pl.BlockSpec((None, D, GATE_PAD), lambda i, j: (l, 0, 0)),
        ],
        out_specs=[
            pl.BlockSpec((tm, W_GRP), lambda i, j: (i, jnp.minimum(j, N_F32 - 1))),
            pl.BlockSpec((tm, W_GRP), lambda i, j: (i, jnp.maximum(j - N_F32, 0))),
            pl.BlockSpec((tm, GATE_PAD), lambda i, j: (i, 0)),
        ],
        out_shape=[
            jax.ShapeDtypeStruct((T, N_F32 * W_GRP), F32),
            jax.ShapeDtypeStruct((T, N_BF16 * W_GRP), BF16),
            jax.ShapeDtypeStruct((T, GATE_PAD), F32),
        ],
        scratch_shapes=[pltpu.VMEM((tm, D), BF16)],
        compiler_params=_params(("arbitrary", "arbitrary")),
        name="proj_in",
    )(x, norm_w.reshape(1, D), w_main, w_gate)


def _causal_mask(L, reverse):
    t = np.arange(L)[:, None]
    s = np.arange(L)[None, :]
    return ((s >= t) if reverse else (s <= t)).astype(np.float32)


def _shift_rows(x, prev_grp, next_grp):
    n, w = x.shape
    ng = n // HALO
    sub = lax.broadcasted_iota(jnp.int32, (1, HALO, 1), 1)
    x3 = jnp.concatenate([prev_grp, x, next_grp], axis=0).reshape(ng + 2, HALO, w)
    down = pltpu.roll(x3, 1, 1)
    up = pltpu.roll(x3, HALO - 1, 1)
    xm1 = jnp.where(sub == 0, down[0:ng], down[1:ng + 1])
    xp1 = jnp.where(sub == HALO - 1, up[2:ng + 2], up[1:ng + 1])
    return xm1.reshape(n, w), xp1.reshape(n, w)


def _dwconv3(x, prev_grp, next_grp, w3, b):
    xm1, xp1 = _shift_rows(x, prev_grp, next_grp)
    return xm1 * w3[0:1] + x * w3[1:2] + xp1 * w3[2:3] + b


def _halo_specs(G, T, width, col, rev):
    nb = T // G
    per = G // HALO
    last = T // HALO - 1

    def blk(c):
        return (nb - 1 - c) if rev else c

    prev = pl.BlockSpec((HALO, width), lambda c: (jnp.maximum(blk(c) * per - 1, 0), col))
    nxt = pl.BlockSpec((HALO, width), lambda c: (jnp.minimum((blk(c) + 1) * per, last), col))
    return prev, nxt


def _mlstm_kernel(*refs, reverse, G, L):
    if reverse:
        (q_ref, qp_ref, qn_ref, k_ref, kp_ref, kn_ref, v_ref, g_ref, gt_ref, gb_ref, gbt_ref,
         cw_ref, cb_ref, mask_ref, hf_ref, o_ref, nw_ref, out_ref,
         qc_ref, kc_ref, ct_ref, m_ref) = refs
    else:
        (q_ref, qp_ref, qn_ref, k_ref, kp_ref, kn_ref, v_ref, g_ref, gt_ref, gb_ref, gbt_ref,
         cw_ref, cb_ref, mask_ref, out_ref,
         qc_ref, kc_ref, ct_ref, m_ref) = refs
    c = pl.program_id(0)
    nblk = pl.num_programs(0)
    blk = (nblk - 1 - c) if reverse else c

    @pl.when(c == 0)
    def _():
        ct_ref[...] = jnp.zeros_like(ct_ref)
        m_ref[...] = jnp.zeros_like(m_ref)

    has_prev = (blk > 0).astype(F32)
    has_next = (blk < nblk - 1).astype(F32)
    cw = cw_ref[...]
    cb = cb_ref[...]
    zq = _dwconv3(q_ref[...], qp_ref[...] * has_prev, qn_ref[...] * has_next, cw[:, :W_GRP], cb[:, :W_GRP])
    qc_ref[...] = zq * _sigmoid(zq)
    zk = _dwconv3(k_ref[...], kp_ref[...] * has_prev, kn_ref[...] * has_next, cw[:, W_GRP:], cb[:, W_GRP:])
    kc_ref[...] = zk * _sigmoid(zk) * (HEAD_DIM ** -0.5)

    maskf = mask_ref[...]
    mask = maskf > 0.5
    ones_col = (lax.broadcasted_iota(jnp.int32, (L, HEAD_DIM), 1) == 0).astype(BF16)
    d = 1 if reverse else 0
    nsub = G // L

    def chunk(j, carry):
        jj = (nsub - 1 - j) if reverse else j
        r0 = pl.multiple_of(jj * L, L)
        rows = pl.ds(r0, L)
        gch = g_ref[rows, :] + gb_ref[...]
        b_cols = _dot(maskf, _log_sigmoid(gch), precision=HIGHEST)
        gt = gt_ref[jj] + gbt_ref[...]
        b_rows = _dot(_log_sigmoid(gt), maskf, _NT, precision=HIGHEST)
        heads = range(N_HEADS)
        hcs = [slice(h * HEAD_DIM, (h + 1) * HEAD_DIM) for h in heads]
        qh = [qc_ref[rows, hc].astype(BF16) for hc in hcs]
        kh = [kc_ref[rows, hc].astype(BF16) for hc in hcs]
        v_aug = [jnp.concatenate([v_ref[rows, hc], ones_col], axis=1) for hc in hcs]
        ct = [ct_ref[h] for h in heads]
        m_prev = [m_ref[h][0:1, 0:1] for h in heads]
        qk = [_dot(qh[h], kh[h], _NT) for h in heads]
        qc = [_dot(qh[h], ct[h].astype(BF16)) for h in heads]
        w_intra, w_inter, m_t, decay, upd = [], [], [], [], []
        for h in heads:
            ci = d * N_HEADS + h
            cf = 2 * N_HEADS + ci
            i_col, b_col = gch[:, ci:ci + 1], b_cols[:, cf:cf + 1]
            i_row, b_row = gt[ci:ci + 1, :], b_rows[cf:cf + 1, :]
            dmat = jnp.where(mask, b_col - b_row + i_row, NEG_BIG)
            inter = b_col + m_prev[h]
            m_t.append(jnp.maximum(inter, jnp.max(dmat, axis=1, keepdims=True)))
            w_intra.append(jnp.where(mask, jnp.exp(dmat - m_t[h]), 0.0))
            w_inter.append(jnp.exp(inter - m_t[h]))
            b_last = b_col[0:1] if reverse else b_col[L - 1:L]
            a_col = i_col + b_last - b_col
            m_new = jnp.maximum(b_last + m_prev[h], jnp.max(a_col, axis=0, keepdims=True))
            decay.append(jnp.exp(b_last + m_prev[h] - m_new))
            wk = jnp.exp(a_col - m_new)
            upd.append(_bdot(kh[h], wk * v_aug[h], _TN))
            m_ref[h] = jnp.broadcast_to(m_new, m_ref.shape[1:])
        sv = [_bdot(qk[h] * w_intra[h], v_aug[h]) for h in heads]
        hh = []
        for h in heads:
            ct_ref[h] = decay[h] * ct[h] + upd[h]
            num = sv[h] + w_inter[h] * qc[h]
            den = num[:, HEAD_DIM:HEAD_DIM + 1]
            hh.append(num[:, :HEAD_DIM] / jnp.maximum(jnp.abs(den), jnp.exp(-m_t[h])))
        if reverse:
            tot = [hf_ref[rows, hcs[h]] + hh[h] for h in heads]
            mu = [_head_mean(tot[h]) for h in heads]
            tc = [tot[h] - mu[h] for h in heads]
            var = [_head_mean(tc[h] * tc[h]) for h in heads]
            for h in heads:
                y = tc[h] * lax.rsqrt(var[h] + EPS) * nw_ref[:, hcs[h]] * _sigmoid(o_ref[rows, hcs[h]])
                out_ref[rows, hcs[h]] = y.astype(out_ref.dtype)
        else:
            for h in heads:
                out_ref[rows, hcs[h]] = hh[h]
        return carry

    lax.fori_loop(0, nsub, chunk, 0, unroll=2)


def _mlstm_dir(proj, proj_b, gates, gates_t, gb, gbt, conv_w, conv_b, reverse, G, hf=None, norm_w=None):
    T = proj.shape[0]
    L = LIN_CHUNK
    nb = T // G

    def blk(c):
        return (nb - 1 - c) if reverse else c

    def col_spec(col):
        return pl.BlockSpec((G, W_GRP), lambda c: (blk(c), col))

    qp, qn = _halo_specs(G, T, W_GRP, COL_A_Q, reverse)
    kp, kn = _halo_specs(G, T, W_GRP, COL_A_K, reverse)
    const2 = lambda shape: pl.BlockSpec(shape, lambda c: (0, 0))
    in_specs = [
        col_spec(COL_A_Q), qp, qn, col_spec(COL_A_K), kp, kn, col_spec(COL_A_V),
        pl.BlockSpec((G, GATE_PAD), lambda c: (blk(c), 0)),
        pl.BlockSpec((G // L, 4 * N_HEADS, L), lambda c: (blk(c), 0, 0)),
        const2((1, GATE_PAD)), const2((4 * N_HEADS, 1)),
        const2((3, 2 * W_GRP)), const2((1, 2 * W_GRP)), const2((L, L)),
    ]
    args = [proj, proj, proj, proj, proj, proj, proj_b, gates, gates_t, gb, gbt,
            conv_w, conv_b.reshape(1, -1), jnp.asarray(_causal_mask(L, reverse))]
    if reverse:
        in_specs += [pl.BlockSpec((G, W_GRP), lambda c: (blk(c), 0)), col_spec(COL_A_O),
                     const2((1, W_GRP))]
        args += [hf, proj, norm_w.reshape(1, -1)]
    return pl.pallas_call(
        functools.partial(_mlstm_kernel, reverse=reverse, G=G, L=L),
        grid=(nb,),
        in_specs=in_specs,
        out_specs=pl.BlockSpec((G, W_GRP), lambda c: (blk(c), 0)),
        out_shape=jax.ShapeDtypeStruct((T, W_GRP), BF16 if reverse else F32),
        scratch_shapes=[
            pltpu.VMEM((G, W_GRP), F32), pltpu.VMEM((G, W_GRP), F32),
            pltpu.VMEM((N_HEADS, HEAD_DIM, 2 * HEAD_DIM), F32),
            pltpu.VMEM((N_HEADS, 8, 128), F32),
        ],
        compiler_params=_params(("arbitrary",)),
        name="mlstm_bwd" if reverse else "mlstm_fwd",
    )(*args)


def _mlstm(proj, proj_b, gates, conv_w, conv_b, i_bias, f_bias, norm_w, G):
    T = proj.shape[0]
    L = LIN_CHUNK
    ng = 4 * N_HEADS
    gbias = jnp.concatenate([i_bias.reshape(-1), f_bias.reshape(-1)]).astype(F32)
    gb = jnp.pad(gbias, (0, GATE_PAD - ng)).reshape(1, GATE_PAD)
    gbt = gbias.reshape(ng, 1)
    gates_t = gates[:, :ng].reshape(T // L, L, ng).transpose(0, 2, 1)
    hf = _mlstm_dir(proj, proj_b, gates, gates_t, gb, gbt, conv_w, conv_b, False, G)
    return _mlstm_dir(proj, proj_b, gates, gates_t, gb, gbt, conv_w, conv_b, True, G, hf, norm_w)


def _hgrn_level_masks(L, reverse):
    n_lev = int(np.log2(L))
    t = np.arange(L)
    lmask = []
    for lev in range(n_lev):
        h = 1 << lev
        blk = t // (2 * h)
        upper = (t // h) % 2 == 1
        is_q = ~upper if reverse else upper
        same = blk[:, None] == blk[None, :]
        lmask.append((same & is_q[:, None] & (~is_q)[None, :]).astype(np.float32))
    lmask.append(np.eye(L, dtype=np.float32))
    return np.stack(lmask)


def _anchor_rows(b, lev, reverse):
    L, W = b.shape
    h = 1 << lev
    a = h if reverse else h - 1
    if 2 * h >= 8:
        b3 = b.reshape(L // (2 * h), 2 * h, W)
        return jnp.broadcast_to(b3[:, a:a + 1, :], b3.shape).reshape(L, W)
    b3 = b.reshape(L // 8, 8, W)
    sub = lax.broadcasted_iota(jnp.int32, (1, 8, 1), 1)
    out = None
    for g in range(8 // (2 * h)):
        piece = jnp.broadcast_to(b3[:, g * 2 * h + a:g * 2 * h + a + 1, :], b3.shape)
        out = piece if out is None else jnp.where(sub >= g * 2 * h, piece, out)
    return out.reshape(L, W)


def _hgrn_kernel(*refs, reverse, G, L):
    if reverse:
        (q_ref, f_ref, i_ref, lb_ref, fb_ref, mask_ref, lmask_ref, of_ref, g_ref, nw_ref,
         out_ref, st_ref) = refs
    else:
        (q_ref, f_ref, i_ref, lb_ref, fb_ref, mask_ref, lmask_ref, out_ref, st_ref) = refs
    c = pl.program_id(0)

    @pl.when(c == 0)
    def _():
        st_ref[...] = jnp.zeros_like(st_ref)

    n_lev = lmask_ref.shape[0] - 1
    nsub = G // L
    heads = range(N_HEADS)
    hcs = [slice(h * HEAD_DIM, (h + 1) * HEAD_DIM) for h in heads]
    t_idx = lax.broadcasted_iota(jnp.int32, (L, 1), 0)

    def chunk(j, carry):
        jj = (nsub - 1 - j) if reverse else j
        r0 = pl.multiple_of(jj * L, L)
        rows = pl.ds(r0, L)
        lb = lb_ref[...]
        z = f_ref[rows, :] + fb_ref[...]
        logf = jnp.log2(lb + (1.0 - lb) * _sigmoid(z))
        key = (1.0 - lb) * _sigmoid(-z)
        qs = q_ref[rows, :]
        qs = qs * _sigmoid(qs)
        b = _dot(mask_ref[...], logf, precision=HIGHEST)
        b_last = b[0:1] if reverse else b[L - 1:L]
        vh = [i_ref[rows, hc].astype(BF16) for hc in hcs]
        st = [st_ref[h] for h in heads]
        q_in = (qs * jnp.exp2(b)).astype(BF16)
        k_out = (key * jnp.exp2(b_last - b)).astype(BF16)
        inter = [_dot(q_in[:, hcs[h]], st[h].astype(BF16), _NT) for h in heads]
        upd = [_dot(vh[h], k_out[:, hcs[h]], _TN) for h in heads]
        qb, kb = qs.astype(BF16), key.astype(BF16)
        scores = [lmask_ref[n_lev] * _dot(qb[:, hcs[h]], kb[:, hcs[h]], _NT) for h in heads]
        for lev in range(n_lev):
            is_q = ((t_idx >> lev) & 1) == (0 if reverse else 1)
            x = (jnp.where(is_q, qs, key) * jnp.exp2(_neg_abs(b - _anchor_rows(b, lev, reverse)))).astype(BF16)
            for h in heads:
                scores[h] = scores[h] + lmask_ref[lev] * _dot(x[:, hcs[h]], x[:, hcs[h]], _NT)
        decay = jnp.exp2(b_last)
        o = []
        for h in heads:
            st_ref[h] = st[h] * decay[:, hcs[h]] + upd[h]
            o.append(_dot(scores[h].astype(BF16), vh[h]) + inter[h])
        if reverse:
            tot = [of_ref[rows, hcs[h]] + o[h] for h in heads]
            for h in heads:
                gg = g_ref[rows, hcs[h]]
                ms = jnp.mean(tot[h] * tot[h], axis=-1, keepdims=True)
                y = tot[h] * lax.rsqrt(ms + EPS) * nw_ref[:, hcs[h]] * (gg * _sigmoid(gg))
                out_ref[rows, hcs[h]] = y.astype(out_ref.dtype)
        else:
            for h in heads:
                out_ref[rows, hcs[h]] = o[h]
        return carry

    lax.fori_loop(0, nsub, chunk, 0, unroll=2)


def _hgrn_dir(proj, proj_b, lb, fb, reverse, G, of=None, norm_w=None):
    T = proj.shape[0]
    L = LIN_CHUNK
    nb = T // G
    mask = _causal_mask(L, reverse)
    lmask = _hgrn_level_masks(L, reverse)

    def blk(c):
        return (nb - 1 - c) if reverse else c

    def col_spec(col):
        return pl.BlockSpec((G, W_GRP), lambda c: (blk(c), col))

    const2 = lambda shape: pl.BlockSpec(shape, lambda c: (0, 0))
    const3 = lambda shape: pl.BlockSpec(shape, lambda c: (0, 0, 0))
    in_specs = [col_spec(COL_B_Q), col_spec(COL_B_FB if reverse else COL_B_FF), col_spec(COL_B_I),
                const2((1, W_GRP)), const2((1, W_GRP)), const2(mask.shape), const3(lmask.shape)]
    args = [proj, proj, proj_b, lb.reshape(1, -1), fb.reshape(1, -1), jnp.asarray(mask), jnp.asarray(lmask)]
    if reverse:
        in_specs += [pl.BlockSpec((G, W_GRP), lambda c: (blk(c), 0)), col_spec(COL_B_G), const2((1, W_GRP))]
        args += [of, proj, norm_w.reshape(1, -1)]
    return pl.pallas_call(
        functools.partial(_hgrn_kernel, reverse=reverse, G=G, L=L),
        grid=(nb,),
        in_specs=in_specs,
        out_specs=pl.BlockSpec((G, W_GRP), lambda c: (blk(c), 0)),
        out_shape=jax.ShapeDtypeStruct((T, W_GRP), BF16 if reverse else F32),
        scratch_shapes=[pltpu.VMEM((N_HEADS, HEAD_DIM, HEAD_DIM), F32)],
        compiler_params=_params(("arbitrary",)),
        name="hgrn_bwd" if reverse else "hgrn_fwd",
    )(*args)


def _hgrn(proj, proj_b, lb, f_bias, norm_w, G):
    of = _hgrn_dir(proj, proj_b, lb, f_bias[0], False, G)
    return _hgrn_dir(proj, proj_b, lb, f_bias[1], True, G, of, norm_w)


def _gelu(x):
    return 0.5 * x * (1.0 + lax.erf(x * (2.0 ** -0.5)))


def _sgu_kernel(u_ref, v_ref, lnw_ref, lnb_ref, ws_ref, bs_ref, nw_ref, out_ref, *, G):
    v = _gelu(v_ref[...])
    mu = jnp.mean(v, axis=-1, keepdims=True)
    vc = v - mu
    var = jnp.mean(vc * vc, axis=-1, keepdims=True)
    vn = (vc * lax.rsqrt(var + EPS) * lnw_ref[...] + lnb_ref[...]).astype(BF16)
    for n in range(G // SGU_CHUNK):
        rows = slice(n * SGU_CHUNK, (n + 1) * SGU_CHUNK)
        for g in range(N_HEADS):
            hc = slice(g * HEAD_DIM, (g + 1) * HEAD_DIM)
            mixed = _dot(ws_ref[g], vn[rows, hc]) + bs_ref[:, g:g + 1]
            y = _gelu(u_ref[rows, hc]) * mixed
            ms = jnp.mean(y * y, axis=-1, keepdims=True)
            out_ref[rows, hc] = (y * lax.rsqrt(ms + EPS) * nw_ref[:, hc]).astype(out_ref.dtype)


def _sgu(proj, ln_w, ln_b, w_s, b_s, norm_w, G):
    T = proj.shape[0]
    row = lambda a: a.reshape(1, -1)
    const2 = lambda shape: pl.BlockSpec(shape, lambda c: (0, 0))
    return pl.pallas_call(
        functools.partial(_sgu_kernel, G=G),
        grid=(T // G,),
        in_specs=[
            pl.BlockSpec((G, W_GRP), lambda c: (c, COL_C_U)),
            pl.BlockSpec((G, W_GRP), lambda c: (c, COL_C_V)),
            const2((1, W_GRP)), const2((1, W_GRP)),
            pl.BlockSpec((N_HEADS, SGU_CHUNK, SGU_CHUNK), lambda c: (0, 0, 0)),
            const2((SGU_CHUNK, N_HEADS)), const2((1, W_GRP)),
        ],
        out_specs=pl.BlockSpec((G, W_GRP), lambda c: (c, 0)),
        out_shape=jax.ShapeDtypeStruct((T, W_GRP), BF16),
        compiler_params=_params(("arbitrary",)),
        name="sgu",
    )(proj, proj, row(ln_w), row(ln_b), w_s.astype(BF16), b_s.T, row(norm_w))


def _na_bias_bands(rel_bias):
    cols = np.arange(GRID_W)
    col_start = np.clip(cols - NA_COLS // 2, 0, GRID_W - NA_COLS)
    cc = cols[None, :]
    valid = (cc >= col_start[:, None]) & (cc < col_start[:, None] + NA_COLS)
    col_off = cc - cols[:, None] + (NA_COLS - 1)
    n_off = 2 * NA_COLS - 1
    onehot = ((col_off[None] == np.arange(n_off)[:, None, None]) & valid[None]).astype(np.float32)
    lead = rel_bias.shape[:-1]
    bands = jnp.dot(rel_bias.astype(F32).reshape(-1, n_off), jnp.asarray(onehot.reshape(n_off, -1)),
                    precision=HIGHEST)
    bands = bands + jnp.asarray(np.where(valid, 0.0, NEG_BIG).astype(np.float32).reshape(1, -1))
    return bands.reshape(*lead, GRID_W, GRID_W)


def _na_kernel(q_ref, k_ref, v_ref, bands_ref, nw_ref, out_ref, bias_ref, *, RB, n_rows):
    R = pl.program_id(1)
    win = NA_ROWS * GRID_W
    scale = HEAD_DIM ** -0.5

    @pl.when(R == 0)
    def _():
        for e in range(NA_ROWS):
            for j in range(NA_ROWS):
                bias_ref[e, :, j * GRID_W:(j + 1) * GRID_W] = bands_ref[j - e + NA_ROWS - 1]

    krows, es = [], []
    for i in range(RB):
        r = R * RB + i
        rs = jnp.clip(r - NA_ROWS // 2, 0, n_rows - NA_ROWS)
        es.append(r - rs)
        krows.append(pl.ds(pl.multiple_of(rs * GRID_W, GRID_W), win))
    qrows = [slice(i * GRID_W, (i + 1) * GRID_W) for i in range(RB)]
    ss = [_bdot(q_ref[qrows[i], :], k_ref[krows[i], :], _NT) for i in range(RB)]
    ps, sums = [], []
    for i in range(RB):
        s = ss[i] * scale + bias_ref[es[i]]
        p = jnp.exp(s - jnp.max(s, axis=-1, keepdims=True))
        sums.append(jnp.sum(p, axis=-1, keepdims=True))
        ps.append(p.astype(BF16))
    os_ = [_dot(ps[i], v_ref[krows[i], :].astype(BF16)) for i in range(RB)]
    for i in range(RB):
        o = os_[i] / sums[i]
        ms = jnp.mean(o * o, axis=-1, keepdims=True)
        out_ref[qrows[i], :] = (o * lax.rsqrt(ms + EPS) * nw_ref[...]).astype(out_ref.dtype)


def _na(proj_b, bands, norm_w, l, RB):
    T = proj_b.shape[0]
    n_rows = T // GRID_W
    cq, ck, cv = (c * N_HEADS for c in (COL_D_Q, COL_D_K, COL_D_V))
    return pl.pallas_call(
        functools.partial(_na_kernel, RB=RB, n_rows=n_rows),
        grid=(N_HEADS, n_rows // RB),
        in_specs=[
            pl.BlockSpec((RB * GRID_W, HEAD_DIM), lambda h, r: (r, cq + h)),
            pl.BlockSpec((T, HEAD_DIM), lambda h, r: (0, ck + h)),
            pl.BlockSpec((T, HEAD_DIM), lambda h, r: (0, cv + h)),
            pl.BlockSpec((None, None, 2 * NA_ROWS - 1, GRID_W, GRID_W), lambda h, r: (l, h, 0, 0, 0)),
            pl.BlockSpec((1, HEAD_DIM), lambda h, r: (0, h)),
        ],
        out_specs=pl.BlockSpec((RB * GRID_W, HEAD_DIM), lambda h, r: (r, h)),
        out_shape=jax.ShapeDtypeStruct((T, W_GRP), BF16),
        scratch_shapes=[pltpu.VMEM((NA_ROWS, GRID_W, NA_ROWS * GRID_W), F32)],
        compiler_params=_params(("arbitrary", "arbitrary")),
        name="na",
    )(proj_b, proj_b, proj_b, bands, norm_w.reshape(1, -1))


def _proj_out_kernel(x_ref, ya_ref, yb_ref, yc_ref, yd_ref, wa_ref, wb_ref, wc_ref, wd_ref, out_ref):
    acc = x_ref[...]
    for y_ref, w_ref in ((ya_ref, wa_ref), (yb_ref, wb_ref), (yc_ref, wc_ref), (yd_ref, wd_ref)):
        acc = acc + _dot(y_ref[...], w_ref[...])
    out_ref[...] = acc


def _proj_out(x, ys, w_out, l, tm, tn):
    T, D = x.shape
    y_spec = pl.BlockSpec((tm, W_GRP), lambda i, j: (i, 0))
    w_specs = [pl.BlockSpec((None, W_GRP, tn), functools.partial(lambda i, j, g: (l, g, j), g=g)) for g in range(4)]
    return pl.pallas_call(
        _proj_out_kernel,
        grid=(T // tm, D // tn),
        in_specs=[pl.BlockSpec((tm, tn), lambda i, j: (i, j))] + [y_spec] * 4 + w_specs,
        out_specs=pl.BlockSpec((tm, tn), lambda i, j: (i, j)),
        out_shape=jax.ShapeDtypeStruct((T, D), F32),
        compiler_params=_params(("arbitrary", "arbitrary")),
        name="proj_out",
    )(x, *ys, w_out, w_out, w_out, w_out)


def _ffn_up_kernel(x_ref, xp_ref, xn_ref, nw_ref, wa_ref, wg_ref, cwa_ref, cwg_ref, cba_ref, cbg_ref,
                   out_ref, h_ref, *, tm):
    i = pl.program_id(0)
    nblk = pl.num_programs(0)
    ng = tm // HALO
    sub = lax.broadcasted_iota(jnp.int32, (1, HALO, 1), 1)

    def conv(z, cw, cb):
        z3 = z.reshape(ng + 2, HALO, z.shape[1])
        down = pltpu.roll(z3, 1, 1)
        up = pltpu.roll(z3, HALO - 1, 1)
        zm1 = jnp.where(sub == 0, down[0:ng], down[1:ng + 1])
        zp1 = jnp.where(sub == HALO - 1, up[2:ng + 2], up[1:ng + 1])
        out = zm1 * cw[0:1] + z3[1:ng + 1] * cw[1:2] + zp1 * cw[2:3] + cb
        return out.reshape(tm, z.shape[1])

    def norm(xf):
        ms = jnp.mean(xf * xf, axis=-1, keepdims=True)
        return xf * lax.rsqrt(ms + EPS) * nw_ref[...]

    @pl.when(pl.program_id(1) == 0)
    def _():
        h_ref[0:HALO, :] = (norm(xp_ref[...]) * (i > 0).astype(F32)).astype(BF16)
        h_ref[HALO:HALO + tm, :] = norm(x_ref[...]).astype(BF16)
        h_ref[HALO + tm:, :] = (norm(xn_ref[...]) * (i < nblk - 1).astype(F32)).astype(BF16)

    h = h_ref[...]
    for s in range(FF_TILE // FF_SUB):
        cs = slice(s * FF_SUB, (s + 1) * FF_SUB)
        a = conv(_dot(h, wa_ref[:, cs]), cwa_ref[:, cs], cba_ref[:, cs])
        g = conv(_dot(h, wg_ref[:, cs]), cwg_ref[:, cs], cbg_ref[:, cs])
        out_ref[:, cs] = (g * _sigmoid(g) * a).astype(out_ref.dtype)


def _ffn_up(x, norm_w, w_up, conv_w, conv_b, l, tm):
    T, D = x.shape
    n_tiles = pl.cdiv(D_FF, FF_TILE)
    per = tm // HALO
    last = T // HALO - 1
    El = pl.Element

    def off(j, base=0):
        return (base // LANE + jnp.minimum(j * (FF_TILE // LANE), (D_FF - FF_TILE) // LANE)) * LANE

    def halves(rows):
        return [pl.BlockSpec((None, El(rows), El(FF_TILE)), lambda i, j: (l, 0, off(j))),
                pl.BlockSpec((None, El(rows), El(FF_TILE)), lambda i, j: (l, 0, off(j, D_FF)))]

    return pl.pallas_call(
        functools.partial(_ffn_up_kernel, tm=tm),
        grid=(T // tm, n_tiles),
        in_specs=[
            pl.BlockSpec((tm, D), lambda i, j: (i, 0)),
            pl.BlockSpec((HALO, D), lambda i, j: (jnp.maximum(i * per - 1, 0), 0)),
            pl.BlockSpec((HALO, D), lambda i, j: (jnp.minimum((i + 1) * per, last), 0)),
            pl.BlockSpec((1, D), lambda i, j: (0, 0)),
        ] + halves(D) + halves(3) + halves(1),
        out_specs=pl.BlockSpec((El(tm), El(FF_TILE)), lambda i, j: (i * tm, off(j))),
        out_shape=jax.ShapeDtypeStruct((T, D_FF), BF16),
        scratch_shapes=[pltpu.VMEM((tm + 2 * HALO, D), BF16)],
        compiler_params=_params(("arbitrary", "arbitrary")),
        name="ffn_up",
    )(x, x, x, norm_w.reshape(1, D), w_up, w_up, conv_w, conv_w, conv_b, conv_b)


def _ffn_down_kernel(x_ref, a_ref, w_ref, out_ref):
    out_ref[...] = x_ref[...] + _dot(a_ref[...], w_ref[...])


def _ffn_down(x, act, w_down, l, tm, tn):
    T, D = x.shape
    K = act.shape[1]
    return pl.pallas_call(
        _ffn_down_kernel,
        grid=(T // tm, D // tn),
        in_specs=[
            pl.BlockSpec((tm, tn), lambda i, j: (i, j)),
            pl.BlockSpec((tm, K), lambda i, j: (i, 0)),
            pl.BlockSpec((None, K, tn), lambda i, j: (l, 0, j)),
        ],
        out_specs=pl.BlockSpec((tm, tn), lambda i, j: (i, j)),
        out_shape=jax.ShapeDtypeStruct((T, D), F32),
        compiler_params=_params(("arbitrary", "arbitrary")),
        name="ffn_down",
    )(x, act, w_down)


def _final_norm_kernel(x_ref, w_ref, out_ref):
    xf = x_ref[...]
    ms = jnp.mean(xf * xf, axis=-1, keepdims=True)
    out_ref[...] = xf * lax.rsqrt(ms + EPS) * w_ref[...]


def _final_norm(x, w, tm):
    T, D = x.shape
    return pl.pallas_call(
        _final_norm_kernel,
        grid=(T // tm,),
        in_specs=[pl.BlockSpec((tm, D), lambda i: (i, 0)), pl.BlockSpec((1, D), lambda i: (0, 0))],
        out_specs=pl.BlockSpec((tm, D), lambda i: (i, 0)),
        out_shape=jax.ShapeDtypeStruct((T, D), F32),
        compiler_params=_params(("arbitrary",)),
        name="final_norm",
    )(x, w.reshape(1, D))


def _split_in_proj(w_in):
    w_main = jnp.concatenate([w_in[..., a:b] for a, b in IN_PROJ_SLICES], axis=-1).astype(BF16)
    n_gate = GATE_SLICE[1] - GATE_SLICE[0]
    zeros = jnp.zeros(w_in.shape[:-1] + (GATE_PAD - n_gate,), BF16)
    w_gate = jnp.concatenate([w_in[..., GATE_SLICE[0]:GATE_SLICE[1]].astype(BF16), zeros], axis=-1)
    return w_main, w_gate


def _tile(T, want):
    return min(want, T)


def kernel(x, norm1_w, w_in, mlstm_conv_w, mlstm_conv_b, mlstm_i_bias, mlstm_f_bias, mlstm_norm_w,
           hgrn_lb_logits, hgrn_f_bias, hgrn_norm_w, sgu_ln_w, sgu_ln_b, sgu_w, sgu_b, na_rel_bias,
           out_norm_w, w_out, norm2_w, ffn_w_up, ffn_conv_w, ffn_conv_b, ffn_w_down, final_norm_w):
    B, T, D = x.shape
    assert B == 1 and D == D_MODEL and T % (NA_ROWS * GRID_W) == 0
    depth = w_in.shape[0]
    tm = _tile(T, 1024)
    G = _tile(T, 512)

    lb_all = jax.nn.softmax(hgrn_lb_logits.astype(F32), axis=0)
    lower_bounds = jnp.cumsum(lb_all, axis=0) - lb_all[0]

    w_main, w_gate = _split_in_proj(w_in)
    w_out_b = w_out.astype(BF16)
    w_up = ffn_w_up.astype(BF16)
    conv_w = ffn_conv_w.astype(F32)
    conv_b = ffn_conv_b.astype(F32)[:, None, :]
    w_dn = ffn_w_down.astype(BF16)
    bands = _na_bias_bands(na_rel_bias)

    xs = x[0]
    for l in range(depth):
        proj, proj_b, gates = _proj_in(xs, norm1_w[l], w_main, w_gate, l, tm)
        y_a = _mlstm(proj, proj_b, gates, mlstm_conv_w[l], mlstm_conv_b[l], mlstm_i_bias[l], mlstm_f_bias[l],
                     mlstm_norm_w[l], G)
        y_b = _hgrn(proj, proj_b, lower_bounds[l], hgrn_f_bias[l], hgrn_norm_w[l], G)
        y_c = _sgu(proj, sgu_ln_w[l], sgu_ln_b[l], sgu_w[l], sgu_b[l], out_norm_w[l, :W_GRP], G)
        y_d = _na(proj_b, bands, out_norm_w[l, W_GRP:], l, NA_ROWS)
        xs = _proj_out(xs, (y_a, y_b, y_c, y_d), w_out_b, l, tm, 1024)
        act = _ffn_up(xs, norm2_w[l], w_up, conv_w, conv_b, l, _tile(T, 512))
        xs = _ffn_down(xs, act, w_dn, l, tm, 512)
    return _final_norm(xs, final_norm_w, tm)[None]
```

```python
import functools

import numpy as np
import jax
import jax.numpy as jnp
from jax import lax
from jax.experimental import pallas as pl
from jax.experimental.pallas import tpu as pltpu

F32 = jnp.float32
BF16 = jnp.bfloat16
HIGHEST = lax.Precision.HIGHEST

D_MODEL = 2048
HEAD_DIM = 128
N_HEADS = 4
W_GRP = N_HEADS * HEAD_DIM
D_FF = 5504
GRID_W = 64
NA_ROWS = 8
NA_COLS = 16
EPS = 1e-6
NEG_BIG = -1e30

LIN_CHUNK = 128
SGU_CHUNK = 128
GATE_PAD = 128
FF_TILE = 512
FF_SUB = 256
FF_DUP = -D_FF % FF_TILE
FF_KC = 256
FF_RB = 64
HALO = 8
LANE = 128

COL_A_Q, COL_A_K = 0, 1
COL_B_Q, COL_B_FF, COL_B_FB, COL_B_G = 2, 3, 4, 5
COL_C_U, COL_C_V = 6, 7
N_F32 = 8
COL_A_V, COL_A_O, COL_B_I, COL_D_Q, COL_D_K, COL_D_V = 0, 1, 2, 3, 4, 5
N_BF16 = 6
IN_PROJ_SLICES = ((0, 1024), (2064, 3600), (4112, 4624), (4624, 5648),
                  (1024, 2048), (3600, 4112), (5648, 7184))
PROJ_TILE = 2 * W_GRP
GATE_SLICE = (2048, 2064)

VMEM_LIMIT = 56 * 1024 * 1024

_NT = (((1,), (1,)), ((), ()))
_TN = (((0,), (0,)), ((), ()))


def _dot(a, b, dims=None, precision=None):
    if dims is None:
        return jnp.dot(a, b, preferred_element_type=F32, precision=precision)
    return lax.dot_general(a, b, dims, preferred_element_type=F32, precision=precision)


def _bdot(a, b, dims=None):
    return _dot(a.astype(BF16), b.astype(BF16), dims)


def _sigmoid(x):
    return jax.nn.sigmoid(x)


def _log_sigmoid(x):
    return jnp.minimum(x, 0.0) - jnp.log1p(jnp.exp(-jnp.abs(x)))


def _neg_abs(x):
    return -jnp.abs(x)


def _head_mean(x):
    hi = x.astype(BF16)
    lo = (x - hi.astype(F32)).astype(BF16)
    avg = jnp.full((HEAD_DIM, HEAD_DIM), 1.0 / HEAD_DIM, BF16)
    return _dot(hi, avg) + _dot(lo, avg)


def _params(sem):
    return pltpu.CompilerParams(dimension_semantics=sem, vmem_limit_bytes=VMEM_LIMIT)


def _proj_in_kernel(x_ref, nw_ref, w_ref, wg_ref, pf_ref, pb_ref, gates_ref, h_ref):
    j = pl.program_id(1)

    @pl.when(j == 0)
    def _():
        xf = x_ref[...]
        ms = jnp.mean(xf * xf, axis=-1, keepdims=True)
        h = (xf * lax.rsqrt(ms + EPS) * nw_ref[...]).astype(BF16)
        h_ref[...] = h
        gates_ref[...] = _dot(h, wg_ref[...])

    n_f32_tiles = N_F32 * W_GRP // PROJ_TILE

    @pl.when(j < n_f32_tiles)
    def _():
        pf_ref[...] = _dot(h_ref[...], w_ref[...])

    @pl.when(j >= n_f32_tiles)
    def _():
        pb_ref[...] = _dot(h_ref[...], w_ref[...]).astype(BF16)


def _proj_in(x, norm_w, w_main, w_gate, l, tm):
    T, D = x.shape
    nf = N_F32 * W_GRP // PROJ_TILE
    nb = N_BF16 * W_GRP // PROJ_TILE
    return pl.pallas_call(
        _proj_in_kernel,
        grid=(T // tm, nf + nb),
        in_specs=[
            pl.BlockSpec((tm, D), lambda i, j: (i, 0)),
            pl.BlockSpec((1, D), lambda i, j: (0, 0)),
            pl.BlockSpec((None, D, PROJ_TILE), lambda i, j: (l, 0, j)),
            pl.BlockSpec((None, D, GATE_PAD), lambda i, j: (l, 0, 0)),
        ],
        out_specs=[
            pl.BlockSpec((tm, PROJ_TILE), lambda i, j: (i, jnp.minimum(j, nf - 1))),
            pl.BlockSpec((tm, PROJ_TILE), lambda i, j: (i, jnp.maximum(j - nf, 0))),
            pl.BlockSpec((tm, GATE_PAD), lambda i, j: (i, 0)),
        ],
        out_shape=[
            jax.ShapeDtypeStruct((T, N_F32 * W_GRP), F32),
            jax.ShapeDtypeStruct((T, N_BF16 * W_GRP), BF16),
            jax.ShapeDtypeStruct((T, GATE_PAD), F32),
        ],
        scratch_shapes=[pltpu.VMEM((tm, D), BF16)],
        compiler_params=_params(("arbitrary", "arbitrary")),
        name="proj_in",
    )(x, norm_w.reshape(1, D), w_main, w_gate)


def _causal_mask(L, reverse):
    t = np.arange(L)[:, None]
    s = np.arange(L)[None, :]
    return ((s >= t) if reverse else (s <= t)).astype(np.float32)


def _shift_rows(x, prev_grp, next_grp):
    n, w = x.shape
    ng = n // HALO
    sub = lax.broadcasted_iota(jnp.int32, (1, HALO, 1), 1)
    x3 = jnp.concatenate([prev_grp, x, next_grp], axis=0).reshape(ng + 2, HALO, w)
    down = pltpu.roll(x3, 1, 1)
    up = pltpu.roll(x3, HALO - 1, 1)
    xm1 = jnp.where(sub == 0, down[0:ng], down[1:ng + 1])
    xp1 = jnp.where(sub == HALO - 1, up[2:ng + 2], up[1:ng + 1])
    return xm1.reshape(n, w), xp1.reshape(n, w)


def _dwconv3(x, prev_grp, next_grp, w3, b):
    xm1, xp1 = _shift_rows(x, prev_grp, next_grp)
    return xm1 * w3[0:1] + x * w3[1:2] + xp1 * w3[2:3] + b


def _halo_specs(G, T, width, col, rev):
    nb = T // G
    per = G // HALO
    last = T // HALO - 1

    def blk(c):
        return (nb - 1 - c) if rev else c

    prev = pl.BlockSpec((HALO, width), lambda c: (jnp.maximum(blk(c) * per - 1, 0), col))
    nxt = pl.BlockSpec((HALO, width), lambda c: (jnp.minimum((blk(c) + 1) * per, last), col))
    return prev, nxt


def _mlstm_kernel(*refs, reverse, G, L):
    if reverse:
        (q_ref, qp_ref, qn_ref, k_ref, kp_ref, kn_ref, v_ref, g_ref, gt_ref, gb_ref, gbt_ref,
         cw_ref, cb_ref, mask_ref, hf_ref, o_ref, nw_ref, out_ref,
         qc_ref, kc_ref, ct_ref, m_ref) = refs
    else:
        (q_ref, qp_ref, qn_ref, k_ref, kp_ref, kn_ref, v_ref, g_ref, gt_ref, gb_ref, gbt_ref,
         cw_ref, cb_ref, mask_ref, out_ref,
         qc_ref, kc_ref, ct_ref, m_ref) = refs
    c = pl.program_id(0)
    nblk = pl.num_programs(0)
    blk = (nblk - 1 - c) if reverse else c

    @pl.when(c == 0)
    def _():
        ct_ref[...] = jnp.zeros_like(ct_ref)
        m_ref[...] = jnp.zeros_like(m_ref)

    has_prev = (blk > 0).astype(F32)
    has_next = (blk < nblk - 1).astype(F32)
    cw = cw_ref[...]
    cb = cb_ref[...]
    zq = _dwconv3(q_ref[...], qp_ref[...] * has_prev, qn_ref[...] * has_next, cw[:, :W_GRP], cb[:, :W_GRP])
    qc_ref[...] = zq * _sigmoid(zq)
    zk = _dwconv3(k_ref[...], kp_ref[...] * has_prev, kn_ref[...] * has_next, cw[:, W_GRP:], cb[:, W_GRP:])
    kc_ref[...] = zk * _sigmoid(zk) * (HEAD_DIM ** -0.5)

    maskf = mask_ref[...]
    mask = maskf > 0.5
    ones_col = (lax.broadcasted_iota(jnp.int32, (L, HEAD_DIM), 1) == 0).astype(BF16)
    d = 1 if reverse else 0
    nsub = G // L

    def chunk(j, carry):
        jj = (nsub - 1 - j) if reverse else j
        r0 = pl.multiple_of(jj * L, L)
        rows = pl.ds(r0, L)
        gch = g_ref[rows, :] + gb_ref[...]
        b_cols = _dot(maskf, _log_sigmoid(gch), precision=HIGHEST)
        gt = gt_ref[jj] + gbt_ref[...]
        b_rows = _dot(_log_sigmoid(gt), maskf, _NT, precision=HIGHEST)
        heads = range(N_HEADS)
        hcs = [slice(h * HEAD_DIM, (h + 1) * HEAD_DIM) for h in heads]
        qh = [qc_ref[rows, hc].astype(BF16) for hc in hcs]
        kh = [kc_ref[rows, hc].astype(BF16) for hc in hcs]
        v_aug = [jnp.concatenate([v_ref[rows, hc], ones_col], axis=1) for hc in hcs]
        ct = [ct_ref[h] for h in heads]
        m_prev = [m_ref[h][0:1, 0:1] for h in heads]
        qk = [_dot(qh[h], kh[h], _NT) for h in heads]
        qc = [_dot(qh[h], ct[h].astype(BF16)) for h in heads]
        w_intra, w_inter, m_t, decay, upd = [], [], [], [], []
        for h in heads:
            ci = d * N_HEADS + h
            cf = 2 * N_HEADS + ci
            i_col, b_col = gch[:, ci:ci + 1], b_cols[:, cf:cf + 1]
            i_row, b_row = gt[ci:ci + 1, :], b_rows[cf:cf + 1, :]
            dmat = jnp.where(mask, b_col - b_row + i_row, NEG_BIG)
            inter = b_col + m_prev[h]
            m_t.append(jnp.maximum(inter, jnp.max(dmat, axis=1, keepdims=True)))
            w_intra.append(jnp.where(mask, jnp.exp(dmat - m_t[h]), 0.0))
            w_inter.append(jnp.exp(inter - m_t[h]))
            b_last = b_col[0:1] if reverse else b_col[L - 1:L]
            a_col = i_col + b_last - b_col
            m_new = jnp.maximum(b_last + m_prev[h], jnp.max(a_col, axis=0, keepdims=True))
            decay.append(jnp.exp(b_last + m_prev[h] - m_new))
            wk = jnp.exp(a_col - m_new)
            upd.append(_bdot(kh[h], wk * v_aug[h], _TN))
            m_ref[h] = jnp.broadcast_to(m_new, m_ref.shape[1:])
        sv = [_bdot(qk[h] * w_intra[h], v_aug[h]) for h in heads]
        hh = []
        for h in heads:
            ct_ref[h] = decay[h] * ct[h] + upd[h]
            num = sv[h] + w_inter[h] * qc[h]
            den = num[:, HEAD_DIM:HEAD_DIM + 1]
            hh.append(num[:, :HEAD_DIM] / jnp.maximum(jnp.abs(den), jnp.exp(-m_t[h])))
        if reverse:
            tot = [hf_ref[rows, hcs[h]] + hh[h] for h in heads]
            mu = [_head_mean(tot[h]) for h in heads]
            tc = [tot[h] - mu[h] for h in heads]
            var = [_head_mean(tc[h] * tc[h]) for h in heads]
            for h in heads:
                y = tc[h] * lax.rsqrt(var[h] + EPS) * nw_ref[:, hcs[h]] * _sigmoid(o_ref[rows, hcs[h]].astype(F32))
                out_ref[rows, hcs[h]] = y.astype(out_ref.dtype)
        else:
            for h in heads:
                out_ref[rows, hcs[h]] = hh[h]
        return carry

    lax.fori_loop(0, nsub, chunk, 0, unroll=2)


def _mlstm_dir(proj, proj_b, gates, gates_t, gb, gbt, conv_w, conv_b, reverse, G, hf=None, norm_w=None):
    T = proj.shape[0]
    L = LIN_CHUNK
    nb = T // G

    def blk(c):
        return (nb - 1 - c) if reverse else c

    def col_spec(col):
        return pl.BlockSpec((G, W_GRP), lambda c: (blk(c), col))

    qp, qn = _halo_specs(G, T, W_GRP, COL_A_Q, reverse)
    kp, kn = _halo_specs(G, T, W_GRP, COL_A_K, reverse)
    const2 = lambda shape: pl.BlockSpec(shape, lambda c: (0, 0))
    in_specs = [
        col_spec(COL_A_Q), qp, qn, col_spec(COL_A_K), kp, kn, col_spec(COL_A_V),
        pl.BlockSpec((G, GATE_PAD), lambda c: (blk(c), 0)),
        pl.BlockSpec((G // L, 4 * N_HEADS, L), lambda c: (blk(c), 0, 0)),
        const2((1, GATE_PAD)), const2((4 * N_HEADS, 1)),
        const2((3, 2 * W_GRP)), const2((1, 2 * W_GRP)), const2((L, L)),
    ]
    args = [proj, proj, proj, proj, proj, proj, proj_b, gates, gates_t, gb, gbt,
            conv_w, conv_b.reshape(1, -1), jnp.asarray(_causal_mask(L, reverse))]
    if reverse:
        in_specs += [pl.BlockSpec((G, W_GRP), lambda c: (blk(c), 0)), col_spec(COL_A_O),
                     const2((1, W_GRP))]
        args += [hf, proj_b, norm_w.reshape(1, -1)]
    return pl.pallas_call(
        functools.partial(_mlstm_kernel, reverse=reverse, G=G, L=L),
        grid=(nb,),
        in_specs=in_specs,
        out_specs=pl.BlockSpec((G, W_GRP), lambda c: (blk(c), 0)),
        out_shape=jax.ShapeDtypeStruct((T, W_GRP), BF16 if reverse else F32),
        scratch_shapes=[
            pltpu.VMEM((G, W_GRP), F32), pltpu.VMEM((G, W_GRP), F32),
            pltpu.VMEM((N_HEADS, HEAD_DIM, 2 * HEAD_DIM), F32),
            pltpu.VMEM((N_HEADS, 8, 128), F32),
        ],
        compiler_params=_params(("arbitrary",)),
        name="mlstm_bwd" if reverse else "mlstm_fwd",
    )(*args)


def _mlstm(proj, proj_b, gates, conv_w, conv_b, i_bias, f_bias, norm_w, G):
    T = proj.shape[0]
    L = LIN_CHUNK
    ng = 4 * N_HEADS
    gbias = jnp.concatenate([i_bias.reshape(-1), f_bias.reshape(-1)]).astype(F32)
    gb = jnp.pad(gbias, (0, GATE_PAD - ng)).reshape(1, GATE_PAD)
    gbt = gbias.reshape(ng, 1)
    gates_t = gates[:, :ng].reshape(T // L, L, ng).transpose(0, 2, 1)
    hf = _mlstm_dir(proj, proj_b, gates, gates_t, gb, gbt, conv_w, conv_b, False, G)
    return _mlstm_dir(proj, proj_b, gates, gates_t, gb, gbt, conv_w, conv_b, True, G, hf, norm_w)


def _hgrn_level_masks(L, reverse):
    n_lev = int(np.log2(L))
    t = np.arange(L)
    lmask = []
    for lev in range(n_lev):
        h = 1 << lev
        blk = t // (2 * h)
        upper = (t // h) % 2 == 1
        is_q = ~upper if reverse else upper
        same = blk[:, None] == blk[None, :]
        lmask.append((same & is_q[:, None] & (~is_q)[None, :]).astype(np.float32))
    lmask.append(np.eye(L, dtype=np.float32))
    return np.stack(lmask)


def _anchor_rows(b, lev, reverse):
    L, W = b.shape
    h = 1 << lev
    a = h if reverse else h - 1
    if 2 * h >= 8:
        b3 = b.reshape(L // (2 * h), 2 * h, W)
        return jnp.broadcast_to(b3[:, a:a + 1, :], b3.shape).reshape(L, W)
    b3 = b.reshape(L // 8, 8, W)
    sub = lax.broadcasted_iota(jnp.int32, (1, 8, 1), 1)
    out = None
    for g in range(8 // (2 * h)):
        piece = jnp.broadcast_to(b3[:, g * 2 * h + a:g * 2 * h + a + 1, :], b3.shape)
        out = piece if out is None else jnp.where(sub >= g * 2 * h, piece, out)
    return out.reshape(L, W)


def _hgrn_kernel(*refs, reverse, G, L):
    if reverse:
        (q_ref, f_ref, i_ref, lb_ref, fb_ref, mask_ref, lmask_ref, of_ref, g_ref, nw_ref,
         out_ref, st_ref) = refs
    else:
        (q_ref, f_ref, i_ref, lb_ref, fb_ref, mask_ref, lmask_ref, out_ref, st_ref) = refs
    c = pl.program_id(0)

    @pl.when(c == 0)
    def _():
        st_ref[...] = jnp.zeros_like(st_ref)

    n_lev = lmask_ref.shape[0] - 1
    nsub = G // L
    heads = range(N_HEADS)
    hcs = [slice(h * HEAD_DIM, (h + 1) * HEAD_DIM) for h in heads]
    t_idx = lax.broadcasted_iota(jnp.int32, (L, 1), 0)

    def chunk(j, carry):
        jj = (nsub - 1 - j) if reverse else j
        r0 = pl.multiple_of(jj * L, L)
        rows = pl.ds(r0, L)
        lb = lb_ref[...]
        z = f_ref[rows, :] + fb_ref[...]
        logf = jnp.log2(lb + (1.0 - lb) * _sigmoid(z))
        key = (1.0 - lb) * _sigmoid(-z)
        qs = q_ref[rows, :]
        qs = qs * _sigmoid(qs)
        b = _dot(mask_ref[...], logf, precision=HIGHEST)
        b_last = b[0:1] if reverse else b[L - 1:L]
        vh = [i_ref[rows, hc].astype(BF16) for hc in hcs]
        st = [st_ref[h] for h in heads]
        q_in = (qs * jnp.exp2(b)).astype(BF16)
        k_out = (key * jnp.exp2(b_last - b)).astype(BF16)
        inter = [_dot(q_in[:, hcs[h]], st[h].astype(BF16), _NT) for h in heads]
        upd = [_dot(vh[h], k_out[:, hcs[h]], _TN) for h in heads]
        qb, kb = qs.astype(BF16), key.astype(BF16)
        scores = [lmask_ref[n_lev] * _dot(qb[:, hcs[h]], kb[:, hcs[h]], _NT) for h in heads]
        for lev in range(n_lev):
            is_q = ((t_idx >> lev) & 1) == (0 if reverse else 1)
            x = (jnp.where(is_q, qs, key) * jnp.exp2(_neg_abs(b - _anchor_rows(b, lev, reverse)))).astype(BF16)
            for h in heads:
                scores[h] = scores[h] + lmask_ref[lev] * _dot(x[:, hcs[h]], x[:, hcs[h]], _NT)
        decay = jnp.exp2(b_last)
        o = []
        for h in heads:
            st_ref[h] = st[h] * decay[:, hcs[h]] + upd[h]
            o.append(_dot(scores[h].astype(BF16), vh[h]) + inter[h])
        if reverse:
            tot = [of_ref[rows, hcs[h]] + o[h] for h in heads]
            for h in heads:
                gg = g_ref[rows, hcs[h]]
                ms = jnp.mean(tot[h] * tot[h], axis=-1, keepdims=True)
                y = tot[h] * lax.rsqrt(ms + EPS) * nw_ref[:, hcs[h]] * (gg * _sigmoid(gg))
                out_ref[rows, hcs[h]] = y.astype(out_ref.dtype)
        else:
            for h in heads:
                out_ref[rows, hcs[h]] = o[h]
        return carry

    lax.fori_loop(0, nsub, chunk, 0, unroll=2)


def _hgrn_dir(proj, proj_b, lb, fb, reverse, G, of=None, norm_w=None):
    T = proj.shape[0]
    L = LIN_CHUNK
    nb = T // G
    mask = _causal_mask(L, reverse)
    lmask = _hgrn_level_masks(L, reverse)

    def blk(c):
        return (nb - 1 - c) if reverse else c

    def col_spec(col):
        return pl.BlockSpec((G, W_GRP), lambda c: (blk(c), col))

    const2 = lambda shape: pl.BlockSpec(shape, lambda c: (0, 0))
    const3 = lambda shape: pl.BlockSpec(shape, lambda c: (0, 0, 0))
    in_specs = [col_spec(COL_B_Q), col_spec(COL_B_FB if reverse else COL_B_FF), col_spec(COL_B_I),
                const2((1, W_GRP)), const2((1, W_GRP)), const2(mask.shape), const3(lmask.shape)]
    args = [proj, proj, proj_b, lb.reshape(1, -1), fb.reshape(1, -1), jnp.asarray(mask), jnp.asarray(lmask)]
    if reverse:
        in_specs += [pl.BlockSpec((G, W_GRP), lambda c: (blk(c), 0)), col_spec(COL_B_G), const2((1, W_GRP))]
        args += [of, proj, norm_w.reshape(1, -1)]
    return pl.pallas_call(
        functools.partial(_hgrn_kernel, reverse=reverse, G=G, L=L),
        grid=(nb,),
        in_specs=in_specs,
        out_specs=pl.BlockSpec((G, W_GRP), lambda c: (blk(c), 0)),
        out_shape=jax.ShapeDtypeStruct((T, W_GRP), BF16 if reverse else F32),
        scratch_shapes=[pltpu.VMEM((N_HEADS, HEAD_DIM, HEAD_DIM), F32)],
        compiler_params=_params(("arbitrary",)),
        name="hgrn_bwd" if reverse else "hgrn_fwd",
    )(*args)


def _hgrn(proj, proj_b, lb, f_bias, norm_w, G):
    of = _hgrn_dir(proj, proj_b, lb, f_bias[0], False, G)
    return _hgrn_dir(proj, proj_b, lb, f_bias[1], True, G, of, norm_w)


def _gelu(x):
    return 0.5 * x * (1.0 + lax.erf(x * (2.0 ** -0.5)))


def _sgu_kernel(u_ref, v_ref, lnw_ref, lnb_ref, ws_ref, bs_ref, nw_ref, out_ref, *, G):
    v = _gelu(v_ref[...])
    mu = jnp.mean(v, axis=-1, keepdims=True)
    vc = v - mu
    var = jnp.mean(vc * vc, axis=-1, keepdims=True)
    vn = (vc * lax.rsqrt(var + EPS) * lnw_ref[...] + lnb_ref[...]).astype(BF16)
    for n in range(G // SGU_CHUNK):
        rows = slice(n * SGU_CHUNK, (n + 1) * SGU_CHUNK)
        for g in range(N_HEADS):
            hc = slice(g * HEAD_DIM, (g + 1) * HEAD_DIM)
            mixed = _dot(ws_ref[g], vn[rows, hc]) + bs_ref[:, g:g + 1]
            y = _gelu(u_ref[rows, hc]) * mixed
            ms = jnp.mean(y * y, axis=-1, keepdims=True)
            out_ref[rows, hc] = (y * lax.rsqrt(ms + EPS) * nw_ref[:, hc]).astype(out_ref.dtype)


def _sgu(proj, ln_w, ln_b, w_s, b_s, norm_w, G):
    T = proj.shape[0]
    row = lambda a: a.reshape(1, -1)
    const2 = lambda shape: pl.BlockSpec(shape, lambda c: (0, 0))
    return pl.pallas_call(
        functools.partial(_sgu_kernel, G=G),
        grid=(T // G,),
        in_specs=[
            pl.BlockSpec((G, W_GRP), lambda c: (c, COL_C_U)),
            pl.BlockSpec((G, W_GRP), lambda c: (c, COL_C_V)),
            const2((1, W_GRP)), const2((1, W_GRP)),
            pl.BlockSpec((N_HEADS, SGU_CHUNK, SGU_CHUNK), lambda c: (0, 0, 0)),
            const2((SGU_CHUNK, N_HEADS)), const2((1, W_GRP)),
        ],
        out_specs=pl.BlockSpec((G, W_GRP), lambda c: (c, 0)),
        out_shape=jax.ShapeDtypeStruct((T, W_GRP), BF16),
        compiler_params=_params(("arbitrary",)),
        name="sgu",
    )(proj, proj, row(ln_w), row(ln_b), w_s.astype(BF16), b_s.T, row(norm_w))


def _na_bias_bands(rel_bias):
    cols = np.arange(GRID_W)
    col_start = np.clip(cols - NA_COLS // 2, 0, GRID_W - NA_COLS)
    cc = cols[None, :]
    valid = (cc >= col_start[:, None]) & (cc < col_start[:, None] + NA_COLS)
    col_off = cc - cols[:, None] + (NA_COLS - 1)
    n_off = 2 * NA_COLS - 1
    onehot = ((col_off[None] == np.arange(n_off)[:, None, None]) & valid[None]).astype(np.float32)
    lead = rel_bias.shape[:-1]
    bands = jnp.dot(rel_bias.astype(F32).reshape(-1, n_off), jnp.asarray(onehot.reshape(n_off, -1)),
                    precision=HIGHEST)
    bands = bands + jnp.asarray(np.where(valid, 0.0, NEG_BIG).astype(np.float32).reshape(1, -1))
    return bands.reshape(*lead, GRID_W, GRID_W)


def _na_kernel(q_ref, k_ref, v_ref, bands_ref, nw_ref, out_ref, bias_ref, *, RB, n_rows):
    R = pl.program_id(1)
    win = NA_ROWS * GRID_W
    scale = HEAD_DIM ** -0.5

    @pl.when(R == 0)
    def _():
        for e in range(NA_ROWS):
            for j in range(NA_ROWS):
                bias_ref[e, :, j * GRID_W:(j + 1) * GRID_W] = bands_ref[j - e + NA_ROWS - 1]

    krows, es = [], []
    for i in range(RB):
        r = R * RB + i
        rs = jnp.clip(r - NA_ROWS // 2, 0, n_rows - NA_ROWS)
        es.append(r - rs)
        krows.append(pl.ds(pl.multiple_of(rs * GRID_W, GRID_W), win))
    qrows = [slice(i * GRID_W, (i + 1) * GRID_W) for i in range(RB)]
    ss = [_bdot(q_ref[qrows[i], :], k_ref[krows[i], :], _NT) for i in range(RB)]
    ps, sums = [], []
    for i in range(RB):
        s = ss[i] * scale + bias_ref[es[i]]
        p = jnp.exp(s - jnp.max(s, axis=-1, keepdims=True))
        sums.append(jnp.sum(p, axis=-1, keepdims=True))
        ps.append(p.astype(BF16))
    os_ = [_dot(ps[i], v_ref[krows[i], :].astype(BF16)) for i in range(RB)]
    for i in range(RB):
        o = os_[i] / sums[i]
        ms = jnp.mean(o * o, axis=-1, keepdims=True)
        out_ref[qrows[i], :] = (o * lax.rsqrt(ms + EPS) * nw_ref[...]).astype(out_ref.dtype)


def _na(proj_b, bands, norm_w, l, RB):
    T = proj_b.shape[0]
    n_rows = T // GRID_W
    cq, ck, cv = (c * N_HEADS for c in (COL_D_Q, COL_D_K, COL_D_V))
    return pl.pallas_call(
        functools.partial(_na_kernel, RB=RB, n_rows=n_rows),
        grid=(N_HEADS, n_rows // RB),
        in_specs=[
            pl.BlockSpec((RB * GRID_W, HEAD_DIM), lambda h, r: (r, cq + h)),
            pl.BlockSpec((T, HEAD_DIM), lambda h, r: (0, ck + h)),
            pl.BlockSpec((T, HEAD_DIM), lambda h, r: (0, cv + h)),
            pl.BlockSpec((None, None, 2 * NA_ROWS - 1, GRID_W, GRID_W), lambda h, r: (l, h, 0, 0, 0)),
            pl.BlockSpec((1, HEAD_DIM), lambda h, r: (0, h)),
        ],
        out_specs=pl.BlockSpec((RB * GRID_W, HEAD_DIM), lambda h, r: (r, h)),
        out_shape=jax.ShapeDtypeStruct((T, W_GRP), BF16),
        scratch_shapes=[pltpu.VMEM((NA_ROWS, GRID_W, NA_ROWS * GRID_W), F32)],
        compiler_params=_params(("arbitrary", "arbitrary")),
        name="na",
    )(proj_b, proj_b, proj_b, bands, norm_w.reshape(1, -1))


def _proj_out_kernel(x_ref, ya_ref, yb_ref, yc_ref, yd_ref, wa_ref, wb_ref, wc_ref, wd_ref, out_ref):
    acc = x_ref[...]
    for y_ref, w_ref in ((ya_ref, wa_ref), (yb_ref, wb_ref), (yc_ref, wc_ref), (yd_ref, wd_ref)):
        acc = acc + _dot(y_ref[...], w_ref[...])
    out_ref[...] = acc


def _proj_out(x, ys, w_out, l, tm, tn):
    T, D = x.shape
    y_spec = pl.BlockSpec((tm, W_GRP), lambda i, j: (i, 0))
    w_specs = [pl.BlockSpec((None, W_GRP, tn), functools.partial(lambda i, j, g: (l, g, j), g=g)) for g in range(4)]
    return pl.pallas_call(
        _proj_out_kernel,
        grid=(T // tm, D // tn),
        in_specs=[pl.BlockSpec((tm, tn), lambda i, j: (i, j))] + [y_spec] * 4 + w_specs,
        out_specs=pl.BlockSpec((tm, tn), lambda i, j: (i, j)),
        out_shape=jax.ShapeDtypeStruct((T, D), F32),
        compiler_params=_params(("arbitrary", "arbitrary")),
        name="proj_out",
    )(x, *ys, w_out, w_out, w_out, w_out)


def _ffn_up_kernel(x_ref, xp_ref, xn_ref, nw_ref, wa_ref, wg_ref, cwa_ref, cwg_ref, cba_ref, cbg_ref,
                   out_ref, h_ref, z0_ref, z1_ref, *, tm, n_tiles, n_steps):
    n = pl.program_id(0)
    nblk = n_steps // n_tiles
    i = jnp.minimum(n // n_tiles, nblk - 1)
    d_model = h_ref.shape[1]
    n_sub = FF_TILE // FF_SUB
    n_k = d_model // FF_KC
    n_rb = tm // FF_RB
    ng = FF_RB // HALO
    sub = lax.broadcasted_iota(jnp.int32, (1, HALO, 1), 1)

    def norm(xf):
        ms = jnp.mean(xf * xf, axis=-1, keepdims=True)
        return xf * lax.rsqrt(ms + EPS) * nw_ref[...]

    @pl.when(n == 0)
    def _():
        z1_ref[...] = jnp.zeros_like(z1_ref)

    @pl.when((lax.rem(n, n_tiles) == 0) & (n < n_steps))
    def _():
        h_ref[0:HALO, :] = (norm(xp_ref[...]) * (i > 0).astype(F32)).astype(BF16)
        h_ref[HALO:HALO + tm, :] = norm(x_ref[...]).astype(BF16)
        h_ref[HALO + tm:, :] = (norm(xn_ref[...]) * (i < nblk - 1).astype(F32)).astype(BF16)

    def finish_rows(zr_ref, s, r):
        cs = slice(s * FF_SUB, (s + 1) * FF_SUB)

        def conv(half, cw, cb):
            z3 = zr_ref[s, half, r * FF_RB:(r + 1) * FF_RB + 2 * HALO, :].reshape(ng + 2, HALO, FF_SUB)
            down = pltpu.roll(z3, 1, 1)
            up = pltpu.roll(z3, HALO - 1, 1)
            zm1 = jnp.where(sub == 0, down[0:ng], down[1:ng + 1])
            zp1 = jnp.where(sub == HALO - 1, up[2:ng + 2], up[1:ng + 1])
            return (zm1 * cw[0:1] + z3[1:ng + 1] * cw[1:2] + zp1 * cw[2:3] + cb).reshape(FF_RB, FF_SUB)

        a = conv(0, cwa_ref[:, cs], cba_ref[:, cs])
        g = conv(1, cwg_ref[:, cs], cbg_ref[:, cs])
        out_ref[r * FF_RB:(r + 1) * FF_RB, cs] = (g * _sigmoid(g) * a).astype(out_ref.dtype)

    def step(zw_ref, zr_ref):
        pieces = [(s, r) for s in range(n_sub) for r in range(n_rb)]
        n_mm = n_sub * 2 * n_k
        done = 0
        for s in range(n_sub):
            cs = slice(s * FF_SUB, (s + 1) * FF_SUB)
            for half, w_ref in ((0, wa_ref), (1, wg_ref)):
                acc = None
                for kc in range(n_k):
                    ks = slice(kc * FF_KC, (kc + 1) * FF_KC)
                    part = _dot(h_ref[:, ks], w_ref[ks, cs])
                    acc = part if acc is None else acc + part
                    m = (s * 2 + half) * n_k + kc + 1
                    while done * n_mm < m * len(pieces):
                        finish_rows(zr_ref, *pieces[done])
                        done += 1
                zw_ref[s, half] = acc

    parity = lax.rem(n, 2)
    pl.when(parity == 0)(lambda: step(z0_ref, z1_ref))
    pl.when(parity == 1)(lambda: step(z1_ref, z0_ref))


def _ffn_up(x, norm_w, w_up, conv_w, conv_b, l, tm):
    T, D = x.shape
    n_tiles = pl.cdiv(D_FF, FF_TILE)
    n_steps = (T // tm) * n_tiles
    per = tm // HALO
    last = T // HALO - 1
    El = pl.Element

    def row(n):
        return jnp.minimum(n // n_tiles, T // tm - 1)

    def col(n):
        return jnp.where(n == n_steps, n_tiles - 1, lax.rem(n, n_tiles))

    def prev(n):
        return jnp.maximum(n - 1, 0)

    def off(j, base=0):
        return (base // LANE + jnp.minimum(j * (FF_TILE // LANE), (D_FF - FF_TILE) // LANE)) * LANE

    def halves(rows, which):
        return [pl.BlockSpec((None, El(rows), El(FF_TILE)), lambda n: (l, 0, off(which(n)))),
                pl.BlockSpec((None, El(rows), El(FF_TILE)), lambda n: (l, 0, off(which(n), D_FF)))]

    prev_col = lambda n: lax.rem(prev(n), n_tiles)
    z_shape = (FF_TILE // FF_SUB, 2, tm + 2 * HALO, FF_SUB)
    return pl.pallas_call(
        functools.partial(_ffn_up_kernel, tm=tm, n_tiles=n_tiles, n_steps=n_steps),
        grid=(n_steps + 1,),
        in_specs=[
            pl.BlockSpec((tm, D), lambda n: (row(n), 0)),
            pl.BlockSpec((HALO, D), lambda n: (jnp.maximum(row(n) * per - 1, 0), 0)),
            pl.BlockSpec((HALO, D), lambda n: (jnp.minimum((row(n) + 1) * per, last), 0)),
            pl.BlockSpec((1, D), lambda n: (0, 0)),
        ] + halves(D, col) + halves(3, prev_col) + halves(1, prev_col),
        out_specs=pl.BlockSpec((tm, FF_TILE), lambda n: (prev(n) // n_tiles, prev_col(n))),
        out_shape=jax.ShapeDtypeStruct((T, n_tiles * FF_TILE), BF16),
        scratch_shapes=[pltpu.VMEM((tm + 2 * HALO, D), BF16), pltpu.VMEM(z_shape, F32), pltpu.VMEM(z_shape, F32)],
        compiler_params=_params(("arbitrary",)),
        name="ffn_up",
    )(x, x, x, norm_w.reshape(1, D), w_up, w_up, conv_w, conv_w, conv_b, conv_b)


def _ffn_down_kernel(x_ref, a_ref, w_ref, out_ref):
    full = (D_FF // FF_TILE) * FF_TILE
    acc = _dot(a_ref[:, :full], w_ref[:full, :]) + _dot(a_ref[:, full + FF_DUP:], w_ref[full:, :])
    out_ref[...] = x_ref[...] + acc


def _ffn_down(x, act, w_down, l, tm, tn):
    T, D = x.shape
    K = act.shape[1]
    return pl.pallas_call(
        _ffn_down_kernel,
        grid=(T // tm, D // tn),
        in_specs=[
            pl.BlockSpec((tm, tn), lambda i, j: (i, j)),
            pl.BlockSpec((tm, K), lambda i, j: (i, 0)),
            pl.BlockSpec((None, D_FF, tn), lambda i, j: (l, 0, j)),
        ],
        out_specs=pl.BlockSpec((tm, tn), lambda i, j: (i, j)),
        out_shape=jax.ShapeDtypeStruct((T, D), F32),
        compiler_params=_params(("arbitrary", "arbitrary")),
        name="ffn_down",
    )(x, act, w_down)


def _final_norm_kernel(x_ref, w_ref, out_ref):
    xf = x_ref[...]
    ms = jnp.mean(xf * xf, axis=-1, keepdims=True)
    out_ref[...] = xf * lax.rsqrt(ms + EPS) * w_ref[...]


def _final_norm(x, w, tm):
    T, D = x.shape
    return pl.pallas_call(
        _final_norm_kernel,
        grid=(T // tm,),
        in_specs=[pl.BlockSpec((tm, D), lambda i: (i, 0)), pl.BlockSpec((1, D), lambda i: (0, 0))],
        out_specs=pl.BlockSpec((tm, D), lambda i: (i, 0)),
        out_shape=jax.ShapeDtypeStruct((T, D), F32),
        compiler_params=_params(("arbitrary",)),
        name="final_norm",
    )(x, w.reshape(1, D))


def _split_in_proj(w_in):
    w_main = jnp.concatenate([w_in[..., a:b] for a, b in IN_PROJ_SLICES], axis=-1).astype(BF16)
    n_gate = GATE_SLICE[1] - GATE_SLICE[0]
    zeros = jnp.zeros(w_in.shape[:-1] + (GATE_PAD - n_gate,), BF16)
    w_gate = jnp.concatenate([w_in[..., GATE_SLICE[0]:GATE_SLICE[1]].astype(BF16), zeros], axis=-1)
    return w_main, w_gate


def _tile(T, want):
    return min(want, T)


def kernel(x, norm1_w, w_in, mlstm_conv_w, mlstm_conv_b, mlstm_i_bias, mlstm_f_bias, mlstm_norm_w,
           hgrn_lb_logits, hgrn_f_bias, hgrn_norm_w, sgu_ln_w, sgu_ln_b, sgu_w, sgu_b, na_rel_bias,
           out_norm_w, w_out, norm2_w, ffn_w_up, ffn_conv_w, ffn_conv_b, ffn_w_down, final_norm_w):
    B, T, D = x.shape
    assert B == 1 and D == D_MODEL and T % (NA_ROWS * GRID_W) == 0
    depth = w_in.shape[0]
    tm = _tile(T, 1024)
    G = _tile(T, 512)

    lb_all = jax.nn.softmax(hgrn_lb_logits.astype(F32), axis=0)
    lower_bounds = jnp.cumsum(lb_all, axis=0) - lb_all[0]

    w_main, w_gate = _split_in_proj(w_in)
    w_out_b = w_out.astype(BF16)
    w_up = ffn_w_up.astype(BF16)
    conv_w = ffn_conv_w.astype(F32)
    conv_b = ffn_conv_b.astype(F32)[:, None, :]
    w_dn = ffn_w_down.astype(BF16)
    bands = _na_bias_bands(na_rel_bias)

    xs = x[0]
    for l in range(depth):
        proj, proj_b, gates = _proj_in(xs, norm1_w[l], w_main, w_gate, l, tm)
        y_a = _mlstm(proj, proj_b, gates, mlstm_conv_w[l], mlstm_conv_b[l], mlstm_i_bias[l], mlstm_f_bias[l],
                     mlstm_norm_w[l], G)
        y_b = _hgrn(proj, proj_b, lower_bounds[l], hgrn_f_bias[l], hgrn_norm_w[l], G)
        y_c = _sgu(proj, sgu_ln_w[l], sgu_ln_b[l], sgu_w[l], sgu_b[l], out_norm_w[l, :W_GRP], G)
        y_d = _na(proj_b, bands, out_norm_w[l, W_GRP:], l, NA_ROWS)
        xs = _proj_out(xs, (y_a, y_b, y_c, y_d), w_out_b, l, tm, 1024)
        act = _ffn_up(xs, norm2_w[l], w_up, conv_w, conv_b, l, _tile(T, 512))
        xs = _ffn_down(xs, act, w_dn, l, tm, 512)
    return _final_norm(xs, final_norm_w, tm)[None]
```

```python
import functools

import numpy as np
import jax
import jax.numpy as jnp
from jax import lax
from jax.experimental import pallas as pl
from jax.experimental.pallas import tpu as pltpu

F32 = jnp.float32
BF16 = jnp.bfloat16
HIGHEST = lax.Precision.HIGHEST

D_MODEL = 2048
HEAD_DIM = 128
N_HEADS = 4
W_GRP = N_HEADS * HEAD_DIM
D_FF = 5504
GRID_W = 64
NA_ROWS = 8
NA_COLS = 16
EPS = 1e-6
NEG_BIG = -1e30

LIN_CHUNK = 128
SGU_CHUNK = 128
GATE_PAD = 128
FF_TILE = 512
FF_SUB = 256
FF_DUP = -D_FF % FF_TILE
FF_MB = 512
FF_KC = 256
FF_RB = 64
HALO = 8
LANE = 128

COL_A_Q, COL_A_K = 0, 1
COL_B_Q, COL_B_FF, COL_B_FB, COL_B_G = 2, 3, 4, 5
COL_C_U, COL_C_V = 6, 7
N_F32 = 8
COL_A_V, COL_A_O, COL_B_I, COL_D_Q, COL_D_K, COL_D_V = 0, 1, 2, 3, 4, 5
N_BF16 = 6
IN_PROJ_SLICES = ((0, 1024), (2064, 3600), (4112, 4624), (4624, 5648),
                  (1024, 2048), (3600, 4112), (5648, 7184))
PROJ_TILE = 2 * W_GRP
GATE_SLICE = (2048, 2064)

VMEM_LIMIT = 56 * 1024 * 1024

_NT = (((1,), (1,)), ((), ()))
_TN = (((0,), (0,)), ((), ()))


def _dot(a, b, dims=None, precision=None):
    if dims is None:
        return jnp.dot(a, b, preferred_element_type=F32, precision=precision)
    return lax.dot_general(a, b, dims, preferred_element_type=F32, precision=precision)


def _bdot(a, b, dims=None):
    return _dot(a.astype(BF16), b.astype(BF16), dims)


def _sigmoid(x):
    return jax.nn.sigmoid(x)


def _log_sigmoid(x):
    return jnp.minimum(x, 0.0) - jnp.log1p(jnp.exp(-jnp.abs(x)))


def _neg_abs(x):
    return -jnp.abs(x)


def _head_mean(x):
    hi = x.astype(BF16)
    lo = (x - hi.astype(F32)).astype(BF16)
    avg = jnp.full((HEAD_DIM, HEAD_DIM), 1.0 / HEAD_DIM, BF16)
    return _dot(hi, avg) + _dot(lo, avg)


def _params(sem):
    return pltpu.CompilerParams(dimension_semantics=sem, vmem_limit_bytes=VMEM_LIMIT)


def _proj_in_kernel(x_ref, nw_ref, w_ref, wg_ref, pf_ref, pb_ref, gates_ref, h_ref):
    j = pl.program_id(1)

    @pl.when(j == 0)
    def _():
        xf = x_ref[...]
        ms = jnp.mean(xf * xf, axis=-1, keepdims=True)
        h = (xf * lax.rsqrt(ms + EPS) * nw_ref[...]).astype(BF16)
        h_ref[...] = h
        gates_ref[...] = _dot(h, wg_ref[...])

    n_f32_tiles = N_F32 * W_GRP // PROJ_TILE

    @pl.when(j < n_f32_tiles)
    def _():
        pf_ref[...] = _dot(h_ref[...], w_ref[...])

    @pl.when(j >= n_f32_tiles)
    def _():
        pb_ref[...] = _dot(h_ref[...], w_ref[...]).astype(BF16)


def _proj_in(x, norm_w, w_main, w_gate, l, tm):
    T, D = x.shape
    nf = N_F32 * W_GRP // PROJ_TILE
    nb = N_BF16 * W_GRP // PROJ_TILE
    return pl.pallas_call(
        _proj_in_kernel,
        grid=(T // tm, nf + nb),
        in_specs=[
            pl.BlockSpec((tm, D), lambda i, j: (i, 0)),
            pl.BlockSpec((1, D), lambda i, j: (0, 0)),
            pl.BlockSpec((None, D, PROJ_TILE), lambda i, j: (l, 0, j)),
            pl.BlockSpec((None, D, GATE_PAD), lambda i, j: (l, 0, 0)),
        ],
        out_specs=[
            pl.BlockSpec((tm, PROJ_TILE), lambda i, j: (i, jnp.minimum(j, nf - 1))),
            pl.BlockSpec((tm, PROJ_TILE), lambda i, j: (i, jnp.maximum(j - nf, 0))),
            pl.BlockSpec((tm, GATE_PAD), lambda i, j: (i, 0)),
        ],
        out_shape=[
            jax.ShapeDtypeStruct((T, N_F32 * W_GRP), F32),
            jax.ShapeDtypeStruct((T, N_BF16 * W_GRP), BF16),
            jax.ShapeDtypeStruct((T, GATE_PAD), F32),
        ],
        scratch_shapes=[pltpu.VMEM((tm, D), BF16)],
        compiler_params=_params(("arbitrary", "arbitrary")),
        name="proj_in",
    )(x, norm_w.reshape(1, D), w_main, w_gate)


def _causal_mask(L, reverse):
    t = np.arange(L)[:, None]
    s = np.arange(L)[None, :]
    return ((s >= t) if reverse else (s <= t)).astype(np.float32)


def _shift_rows(x, prev_grp, next_grp):
    n, w = x.shape
    ng = n // HALO
    sub = lax.broadcasted_iota(jnp.int32, (1, HALO, 1), 1)
    x3 = jnp.concatenate([prev_grp, x, next_grp], axis=0).reshape(ng + 2, HALO, w)
    down = pltpu.roll(x3, 1, 1)
    up = pltpu.roll(x3, HALO - 1, 1)
    xm1 = jnp.where(sub == 0, down[0:ng], down[1:ng + 1])
    xp1 = jnp.where(sub == HALO - 1, up[2:ng + 2], up[1:ng + 1])
    return xm1.reshape(n, w), xp1.reshape(n, w)


def _dwconv3(x, prev_grp, next_grp, w3, b):
    xm1, xp1 = _shift_rows(x, prev_grp, next_grp)
    return xm1 * w3[0:1] + x * w3[1:2] + xp1 * w3[2:3] + b


def _halo_specs(G, T, width, col, rev):
    nb = T // G
    per = G // HALO
    last = T // HALO - 1

    def blk(c):
        return (nb - 1 - c) if rev else c

    prev = pl.BlockSpec((HALO, width), lambda c: (jnp.maximum(blk(c) * per - 1, 0), col))
    nxt = pl.BlockSpec((HALO, width), lambda c: (jnp.minimum((blk(c) + 1) * per, last), col))
    return prev, nxt


def _mlstm_kernel(*refs, reverse, G, L):
    if reverse:
        (q_ref, qp_ref, qn_ref, k_ref, kp_ref, kn_ref, v_ref, g_ref, gt_ref, gb_ref, gbt_ref,
         cw_ref, cb_ref, mask_ref, hf_ref, o_ref, nw_ref, out_ref,
         qc_ref, kc_ref, ct_ref, m_ref) = refs
    else:
        (q_ref, qp_ref, qn_ref, k_ref, kp_ref, kn_ref, v_ref, g_ref, gt_ref, gb_ref, gbt_ref,
         cw_ref, cb_ref, mask_ref, out_ref,
         qc_ref, kc_ref, ct_ref, m_ref) = refs
    c = pl.program_id(0)
    nblk = pl.num_programs(0)
    blk = (nblk - 1 - c) if reverse else c

    @pl.when(c == 0)
    def _():
        ct_ref[...] = jnp.zeros_like(ct_ref)
        m_ref[...] = jnp.zeros_like(m_ref)

    has_prev = (blk > 0).astype(F32)
    has_next = (blk < nblk - 1).astype(F32)
    cw = cw_ref[...]
    cb = cb_ref[...]
    zq = _dwconv3(q_ref[...], qp_ref[...] * has_prev, qn_ref[...] * has_next, cw[:, :W_GRP], cb[:, :W_GRP])
    qc_ref[...] = zq * _sigmoid(zq)
    zk = _dwconv3(k_ref[...], kp_ref[...] * has_prev, kn_ref[...] * has_next, cw[:, W_GRP:], cb[:, W_GRP:])
    kc_ref[...] = zk * _sigmoid(zk) * (HEAD_DIM ** -0.5)

    maskf = mask_ref[...]
    mask = maskf > 0.5
    ones_col = (lax.broadcasted_iota(jnp.int32, (L, HEAD_DIM), 1) == 0).astype(BF16)
    d = 1 if reverse else 0
    nsub = G // L

    def chunk(j, carry):
        jj = (nsub - 1 - j) if reverse else j
        r0 = pl.multiple_of(jj * L, L)
        rows = pl.ds(r0, L)
        gch = g_ref[rows, :] + gb_ref[...]
        b_cols = _dot(maskf, _log_sigmoid(gch), precision=HIGHEST)
        gt = gt_ref[jj] + gbt_ref[...]
        b_rows = _dot(_log_sigmoid(gt), maskf, _NT, precision=HIGHEST)
        heads = range(N_HEADS)
        hcs = [slice(h * HEAD_DIM, (h + 1) * HEAD_DIM) for h in heads]
        qh = [qc_ref[rows, hc].astype(BF16) for hc in hcs]
        kh = [kc_ref[rows, hc].astype(BF16) for hc in hcs]
        v_aug = [jnp.concatenate([v_ref[rows, hc], ones_col], axis=1) for hc in hcs]
        ct = [ct_ref[h] for h in heads]
        m_prev = [m_ref[h][0:1, 0:1] for h in heads]
        qk = [_dot(qh[h], kh[h], _NT) for h in heads]
        qc = [_dot(qh[h], ct[h].astype(BF16)) for h in heads]
        w_intra, w_inter, m_t, decay, upd = [], [], [], [], []
        for h in heads:
            ci = d * N_HEADS + h
            cf = 2 * N_HEADS + ci
            i_col, b_col = gch[:, ci:ci + 1], b_cols[:, cf:cf + 1]
            i_row, b_row = gt[ci:ci + 1, :], b_rows[cf:cf + 1, :]
            dmat = jnp.where(mask, b_col - b_row + i_row, NEG_BIG)
            inter = b_col + m_prev[h]
            m_t.append(jnp.maximum(inter, jnp.max(dmat, axis=1, keepdims=True)))
            w_intra.append(jnp.where(mask, jnp.exp(dmat - m_t[h]), 0.0))
            w_inter.append(jnp.exp(inter - m_t[h]))
            b_last = b_col[0:1] if reverse else b_col[L - 1:L]
            a_col = i_col + b_last - b_col
            m_new = jnp.maximum(b_last + m_prev[h], jnp.max(a_col, axis=0, keepdims=True))
            decay.append(jnp.exp(b_last + m_prev[h] - m_new))
            wk = jnp.exp(a_col - m_new)
            upd.append(_bdot(kh[h], wk * v_aug[h], _TN))
            m_ref[h] = jnp.broadcast_to(m_new, m_ref.shape[1:])
        sv = [_bdot(qk[h] * w_intra[h], v_aug[h]) for h in heads]
        hh = []
        for h in heads:
            ct_ref[h] = decay[h] * ct[h] + upd[h]
            num = sv[h] + w_inter[h] * qc[h]
            den = num[:, HEAD_DIM:HEAD_DIM + 1]
            hh.append(num[:, :HEAD_DIM] / jnp.maximum(jnp.abs(den), jnp.exp(-m_t[h])))
        if reverse:
            tot = [hf_ref[rows, hcs[h]] + hh[h] for h in heads]
            mu = [_head_mean(tot[h]) for h in heads]
            tc = [tot[h] - mu[h] for h in heads]
            var = [_head_mean(tc[h] * tc[h]) for h in heads]
            for h in heads:
                y = tc[h] * lax.rsqrt(var[h] + EPS) * nw_ref[:, hcs[h]] * _sigmoid(o_ref[rows, hcs[h]].astype(F32))
                out_ref[rows, hcs[h]] = y.astype(out_ref.dtype)
        else:
            for h in heads:
                out_ref[rows, hcs[h]] = hh[h]
        return carry

    lax.fori_loop(0, nsub, chunk, 0, unroll=2)


def _mlstm_dir(proj, proj_b, gates, gates_t, gb, gbt, conv_w, conv_b, reverse, G, hf=None, norm_w=None):
    T = proj.shape[0]
    L = LIN_CHUNK
    nb = T // G

    def blk(c):
        return (nb - 1 - c) if reverse else c

    def col_spec(col):
        return pl.BlockSpec((G, W_GRP), lambda c: (blk(c), col))

    qp, qn = _halo_specs(G, T, W_GRP, COL_A_Q, reverse)
    kp, kn = _halo_specs(G, T, W_GRP, COL_A_K, reverse)
    const2 = lambda shape: pl.BlockSpec(shape, lambda c: (0, 0))
    in_specs = [
        col_spec(COL_A_Q), qp, qn, col_spec(COL_A_K), kp, kn, col_spec(COL_A_V),
        pl.BlockSpec((G, GATE_PAD), lambda c: (blk(c), 0)),
        pl.BlockSpec((G // L, 4 * N_HEADS, L), lambda c: (blk(c), 0, 0)),
        const2((1, GATE_PAD)), const2((4 * N_HEADS, 1)),
        const2((3, 2 * W_GRP)), const2((1, 2 * W_GRP)), const2((L, L)),
    ]
    args = [proj, proj, proj, proj, proj, proj, proj_b, gates, gates_t, gb, gbt,
            conv_w, conv_b.reshape(1, -1), jnp.asarray(_causal_mask(L, reverse))]
    if reverse:
        in_specs += [pl.BlockSpec((G, W_GRP), lambda c: (blk(c), 0)), col_spec(COL_A_O),
                     const2((1, W_GRP))]
        args += [hf, proj_b, norm_w.reshape(1, -1)]
    return pl.pallas_call(
        functools.partial(_mlstm_kernel, reverse=reverse, G=G, L=L),
        grid=(nb,),
        in_specs=in_specs,
        out_specs=pl.BlockSpec((G, W_GRP), lambda c: (blk(c), 0)),
        out_shape=jax.ShapeDtypeStruct((T, W_GRP), BF16 if reverse else F32),
        scratch_shapes=[
            pltpu.VMEM((G, W_GRP), F32), pltpu.VMEM((G, W_GRP), F32),
            pltpu.VMEM((N_HEADS, HEAD_DIM, 2 * HEAD_DIM), F32),
            pltpu.VMEM((N_HEADS, 8, 128), F32),
        ],
        compiler_params=_params(("arbitrary",)),
        name="mlstm_bwd" if reverse else "mlstm_fwd",
    )(*args)


def _mlstm(proj, proj_b, gates, conv_w, conv_b, i_bias, f_bias, norm_w, G):
    T = proj.shape[0]
    L = LIN_CHUNK
    ng = 4 * N_HEADS
    gbias = jnp.concatenate([i_bias.reshape(-1), f_bias.reshape(-1)]).astype(F32)
    gb = jnp.pad(gbias, (0, GATE_PAD - ng)).reshape(1, GATE_PAD)
    gbt = gbias.reshape(ng, 1)
    gates_t = gates[:, :ng].reshape(T // L, L, ng).transpose(0, 2, 1)
    hf = _mlstm_dir(proj, proj_b, gates, gates_t, gb, gbt, conv_w, conv_b, False, G)
    return _mlstm_dir(proj, proj_b, gates, gates_t, gb, gbt, conv_w, conv_b, True, G, hf, norm_w)


def _hgrn_level_masks(L, reverse):
    n_lev = int(np.log2(L))
    t = np.arange(L)
    lmask = []
    for lev in range(n_lev):
        h = 1 << lev
        blk = t // (2 * h)
        upper = (t // h) % 2 == 1
        is_q = ~upper if reverse else upper
        same = blk[:, None] == blk[None, :]
        lmask.append((same & is_q[:, None] & (~is_q)[None, :]).astype(np.float32))
    lmask.append(np.eye(L, dtype=np.float32))
    return np.stack(lmask)


def _anchor_rows(b, lev, reverse):
    L, W = b.shape
    h = 1 << lev
    a = h if reverse else h - 1
    if 2 * h >= 8:
        b3 = b.reshape(L // (2 * h), 2 * h, W)
        return jnp.broadcast_to(b3[:, a:a + 1, :], b3.shape).reshape(L, W)
    b3 = b.reshape(L // 8, 8, W)
    sub = lax.broadcasted_iota(jnp.int32, (1, 8, 1), 1)
    out = None
    for g in range(8 // (2 * h)):
        piece = jnp.broadcast_to(b3[:, g * 2 * h + a:g * 2 * h + a + 1, :], b3.shape)
        out = piece if out is None else jnp.where(sub >= g * 2 * h, piece, out)
    return out.reshape(L, W)


def _hgrn_kernel(*refs, reverse, G, L):
    if reverse:
        (q_ref, f_ref, i_ref, lb_ref, fb_ref, mask_ref, lmask_ref, of_ref, g_ref, nw_ref,
         out_ref, st_ref) = refs
    else:
        (q_ref, f_ref, i_ref, lb_ref, fb_ref, mask_ref, lmask_ref, out_ref, st_ref) = refs
    c = pl.program_id(0)

    @pl.when(c == 0)
    def _():
        st_ref[...] = jnp.zeros_like(st_ref)

    n_lev = lmask_ref.shape[0] - 1
    nsub = G // L
    heads = range(N_HEADS)
    hcs = [slice(h * HEAD_DIM, (h + 1) * HEAD_DIM) for h in heads]
    t_idx = lax.broadcasted_iota(jnp.int32, (L, 1), 0)

    def chunk(j, carry):
        jj = (nsub - 1 - j) if reverse else j
        r0 = pl.multiple_of(jj * L, L)
        rows = pl.ds(r0, L)
        lb = lb_ref[...]
        z = f_ref[rows, :] + fb_ref[...]
        logf = jnp.log2(lb + (1.0 - lb) * _sigmoid(z))
        key = (1.0 - lb) * _sigmoid(-z)
        qs = q_ref[rows, :]
        qs = qs * _sigmoid(qs)
        b = _dot(mask_ref[...], logf, precision=HIGHEST)
        b_last = b[0:1] if reverse else b[L - 1:L]
        vh = [i_ref[rows, hc].astype(BF16) for hc in hcs]
        st = [st_ref[h] for h in heads]
        q_in = (qs * jnp.exp2(b)).astype(BF16)
        k_out = (key * jnp.exp2(b_last - b)).astype(BF16)
        inter = [_dot(q_in[:, hcs[h]], st[h].astype(BF16), _NT) for h in heads]
        upd = [_dot(vh[h], k_out[:, hcs[h]], _TN) for h in heads]
        qb, kb = qs.astype(BF16), key.astype(BF16)
        scores = [lmask_ref[n_lev] * _dot(qb[:, hcs[h]], kb[:, hcs[h]], _NT) for h in heads]
        for lev in range(n_lev):
            is_q = ((t_idx >> lev) & 1) == (0 if reverse else 1)
            x = (jnp.where(is_q, qs, key) * jnp.exp2(_neg_abs(b - _anchor_rows(b, lev, reverse)))).astype(BF16)
            for h in heads:
                scores[h] = scores[h] + lmask_ref[lev] * _dot(x[:, hcs[h]], x[:, hcs[h]], _NT)
        decay = jnp.exp2(b_last)
        o = []
        for h in heads:
            st_ref[h] = st[h] * decay[:, hcs[h]] + upd[h]
            o.append(_dot(scores[h].astype(BF16), vh[h]) + inter[h])
        if reverse:
            tot = [of_ref[rows, hcs[h]] + o[h] for h in heads]
            for h in heads:
                gg = g_ref[rows, hcs[h]]
                ms = jnp.mean(tot[h] * tot[h], axis=-1, keepdims=True)
                y = tot[h] * lax.rsqrt(ms + EPS) * nw_ref[:, hcs[h]] * (gg * _sigmoid(gg))
                out_ref[rows, hcs[h]] = y.astype(out_ref.dtype)
        else:
            for h in heads:
                out_ref[rows, hcs[h]] = o[h]
        return carry

    lax.fori_loop(0, nsub, chunk, 0, unroll=2)


def _hgrn_dir(proj, proj_b, lb, fb, reverse, G, of=None, norm_w=None):
    T = proj.shape[0]
    L = LIN_CHUNK
    nb = T // G
    mask = _causal_mask(L, reverse)
    lmask = _hgrn_level_masks(L, reverse)

    def blk(c):
        return (nb - 1 - c) if reverse else c

    def col_spec(col):
        return pl.BlockSpec((G, W_GRP), lambda c: (blk(c), col))

    const2 = lambda shape: pl.BlockSpec(shape, lambda c: (0, 0))
    const3 = lambda shape: pl.BlockSpec(shape, lambda c: (0, 0, 0))
    in_specs = [col_spec(COL_B_Q), col_spec(COL_B_FB if reverse else COL_B_FF), col_spec(COL_B_I),
                const2((1, W_GRP)), const2((1, W_GRP)), const2(mask.shape), const3(lmask.shape)]
    args = [proj, proj, proj_b, lb.reshape(1, -1), fb.reshape(1, -1), jnp.asarray(mask), jnp.asarray(lmask)]
    if reverse:
        in_specs += [pl.BlockSpec((G, W_GRP), lambda c: (blk(c), 0)), col_spec(COL_B_G), const2((1, W_GRP))]
        args += [of, proj, norm_w.reshape(1, -1)]
    return pl.pallas_call(
        functools.partial(_hgrn_kernel, reverse=reverse, G=G, L=L),
        grid=(nb,),
        in_specs=in_specs,
        out_specs=pl.BlockSpec((G, W_GRP), lambda c: (blk(c), 0)),
        out_shape=jax.ShapeDtypeStruct((T, W_GRP), BF16 if reverse else F32),
        scratch_shapes=[pltpu.VMEM((N_HEADS, HEAD_DIM, HEAD_DIM), F32)],
        compiler_params=_params(("arbitrary",)),
        name="hgrn_bwd" if reverse else "hgrn_fwd",
    )(*args)


def _hgrn(proj, proj_b, lb, f_bias, norm_w, G):
    of = _hgrn_dir(proj, proj_b, lb, f_bias[0], False, G)
    return _hgrn_dir(proj, proj_b, lb, f_bias[1], True, G, of, norm_w)


def _gelu(x):
    return 0.5 * x * (1.0 + lax.erf(x * (2.0 ** -0.5)))


def _sgu_kernel(u_ref, v_ref, lnw_ref, lnb_ref, ws_ref, bs_ref, nw_ref, out_ref, *, G):
    v = _gelu(v_ref[...])
    mu = jnp.mean(v, axis=-1, keepdims=True)
    vc = v - mu
    var = jnp.mean(vc * vc, axis=-1, keepdims=True)
    vn = (vc * lax.rsqrt(var + EPS) * lnw_ref[...] + lnb_ref[...]).astype(BF16)
    for n in range(G // SGU_CHUNK):
        rows = slice(n * SGU_CHUNK, (n + 1) * SGU_CHUNK)
        for g in range(N_HEADS):
            hc = slice(g * HEAD_DIM, (g + 1) * HEAD_DIM)
            mixed = _dot(ws_ref[g], vn[rows, hc]) + bs_ref[:, g:g + 1]
            y = _gelu(u_ref[rows, hc]) * mixed
            ms = jnp.mean(y * y, axis=-1, keepdims=True)
            out_ref[rows, hc] = (y * lax.rsqrt(ms + EPS) * nw_ref[:, hc]).astype(out_ref.dtype)


def _sgu(proj, ln_w, ln_b, w_s, b_s, norm_w, G):
    T = proj.shape[0]
    row = lambda a: a.reshape(1, -1)
    const2 = lambda shape: pl.BlockSpec(shape, lambda c: (0, 0))
    return pl.pallas_call(
        functools.partial(_sgu_kernel, G=G),
        grid=(T // G,),
        in_specs=[
            pl.BlockSpec((G, W_GRP), lambda c: (c, COL_C_U)),
            pl.BlockSpec((G, W_GRP), lambda c: (c, COL_C_V)),
            const2((1, W_GRP)), const2((1, W_GRP)),
            pl.BlockSpec((N_HEADS, SGU_CHUNK, SGU_CHUNK), lambda c: (0, 0, 0)),
            const2((SGU_CHUNK, N_HEADS)), const2((1, W_GRP)),
        ],
        out_specs=pl.BlockSpec((G, W_GRP), lambda c: (c, 0)),
        out_shape=jax.ShapeDtypeStruct((T, W_GRP), BF16),
        compiler_params=_params(("arbitrary",)),
        name="sgu",
    )(proj, proj, row(ln_w), row(ln_b), w_s.astype(BF16), b_s.T, row(norm_w))


def _na_bias_bands(rel_bias):
    cols = np.arange(GRID_W)
    col_start = np.clip(cols - NA_COLS // 2, 0, GRID_W - NA_COLS)
    cc = cols[None, :]
    valid = (cc >= col_start[:, None]) & (cc < col_start[:, None] + NA_COLS)
    col_off = cc - cols[:, None] + (NA_COLS - 1)
    n_off = 2 * NA_COLS - 1
    onehot = ((col_off[None] == np.arange(n_off)[:, None, None]) & valid[None]).astype(np.float32)
    lead = rel_bias.shape[:-1]
    bands = jnp.dot(rel_bias.astype(F32).reshape(-1, n_off), jnp.asarray(onehot.reshape(n_off, -1)),
                    precision=HIGHEST)
    bands = bands + jnp.asarray(np.where(valid, 0.0, NEG_BIG).astype(np.float32).reshape(1, -1))
    return bands.reshape(*lead, GRID_W, GRID_W)


def _na_kernel(q_ref, k_ref, v_ref, bands_ref, nw_ref, out_ref, bias_ref, *, RB, n_rows):
    R = pl.program_id(1)
    win = NA_ROWS * GRID_W
    scale = HEAD_DIM ** -0.5

    @pl.when(R == 0)
    def _():
        for e in range(NA_ROWS):
            for j in range(NA_ROWS):
                bias_ref[e, :, j * GRID_W:(j + 1) * GRID_W] = bands_ref[j - e + NA_ROWS - 1]

    krows, es = [], []
    for i in range(RB):
        r = R * RB + i
        rs = jnp.clip(r - NA_ROWS // 2, 0, n_rows - NA_ROWS)
        es.append(r - rs)
        krows.append(pl.ds(pl.multiple_of(rs * GRID_W, GRID_W), win))
    qrows = [slice(i * GRID_W, (i + 1) * GRID_W) for i in range(RB)]
    ss = [_bdot(q_ref[qrows[i], :], k_ref[krows[i], :], _NT) for i in range(RB)]
    ps, sums = [], []
    for i in range(RB):
        s = ss[i] * scale + bias_ref[es[i]]
        p = jnp.exp(s - jnp.max(s, axis=-1, keepdims=True))
        sums.append(jnp.sum(p, axis=-1, keepdims=True))
        ps.append(p.astype(BF16))
    os_ = [_dot(ps[i], v_ref[krows[i], :].astype(BF16)) for i in range(RB)]
    for i in range(RB):
        o = os_[i] / sums[i]
        ms = jnp.mean(o * o, axis=-1, keepdims=True)
        out_ref[qrows[i], :] = (o * lax.rsqrt(ms + EPS) * nw_ref[...]).astype(out_ref.dtype)


def _na(proj_b, bands, norm_w, l, RB):
    T = proj_b.shape[0]
    n_rows = T // GRID_W
    cq, ck, cv = (c * N_HEADS for c in (COL_D_Q, COL_D_K, COL_D_V))
    return pl.pallas_call(
        functools.partial(_na_kernel, RB=RB, n_rows=n_rows),
        grid=(N_HEADS, n_rows // RB),
        in_specs=[
            pl.BlockSpec((RB * GRID_W, HEAD_DIM), lambda h, r: (r, cq + h)),
            pl.BlockSpec((T, HEAD_DIM), lambda h, r: (0, ck + h)),
            pl.BlockSpec((T, HEAD_DIM), lambda h, r: (0, cv + h)),
            pl.BlockSpec((None, None, 2 * NA_ROWS - 1, GRID_W, GRID_W), lambda h, r: (l, h, 0, 0, 0)),
            pl.BlockSpec((1, HEAD_DIM), lambda h, r: (0, h)),
        ],
        out_specs=pl.BlockSpec((RB * GRID_W, HEAD_DIM), lambda h, r: (r, h)),
        out_shape=jax.ShapeDtypeStruct((T, W_GRP), BF16),
        scratch_shapes=[pltpu.VMEM((NA_ROWS, GRID_W, NA_ROWS * GRID_W), F32)],
        compiler_params=_params(("arbitrary", "arbitrary")),
        name="na",
    )(proj_b, proj_b, proj_b, bands, norm_w.reshape(1, -1))


def _proj_out_kernel(x_ref, ya_ref, yb_ref, yc_ref, yd_ref, wa_ref, wb_ref, wc_ref, wd_ref, out_ref):
    acc = x_ref[...]
    for y_ref, w_ref in ((ya_ref, wa_ref), (yb_ref, wb_ref), (yc_ref, wc_ref), (yd_ref, wd_ref)):
        acc = acc + _dot(y_ref[...], w_ref[...])
    out_ref[...] = acc


def _proj_out(x, ys, w_out, l, tm, tn):
    T, D = x.shape
    y_spec = pl.BlockSpec((tm, W_GRP), lambda i, j: (i, 0))
    w_specs = [pl.BlockSpec((None, W_GRP, tn), functools.partial(lambda i, j, g: (l, g, j), g=g)) for g in range(4)]
    return pl.pallas_call(
        _proj_out_kernel,
        grid=(T // tm, D // tn),
        in_specs=[pl.BlockSpec((tm, tn), lambda i, j: (i, j))] + [y_spec] * 4 + w_specs,
        out_specs=pl.BlockSpec((tm, tn), lambda i, j: (i, j)),
        out_shape=jax.ShapeDtypeStruct((T, D), F32),
        compiler_params=_params(("arbitrary", "arbitrary")),
        name="proj_out",
    )(x, *ys, w_out, w_out, w_out, w_out)


def _ffn_up_kernel(x_ref, xp_ref, xn_ref, nw_ref, wa_ref, wg_ref, cwa_ref, cwg_ref, cba_ref, cbg_ref,
                   out_ref, h_ref, z0_ref, z1_ref, *, tm, n_tiles, n_steps):
    n = pl.program_id(0)
    nblk = n_steps // n_tiles
    i = jnp.minimum(n // n_tiles, nblk - 1)
    d_model = h_ref.shape[1]
    n_mb = tm // FF_MB
    n_sub = FF_TILE // FF_SUB
    n_k = d_model // FF_KC
    n_rb = FF_MB // FF_RB
    ng = FF_RB // HALO
    sub = lax.broadcasted_iota(jnp.int32, (1, HALO, 1), 1)

    def norm(xf):
        ms = jnp.mean(xf * xf, axis=-1, keepdims=True)
        return xf * lax.rsqrt(ms + EPS) * nw_ref[...]

    @pl.when(n == 0)
    def _():
        z1_ref[...] = jnp.zeros_like(z1_ref)

    @pl.when((lax.rem(n, n_tiles) == 0) & (n < n_steps))
    def _():
        h_ref[0:HALO, :] = (norm(xp_ref[...]) * (i > 0).astype(F32)).astype(BF16)
        h_ref[HALO:HALO + tm, :] = norm(x_ref[...]).astype(BF16)
        h_ref[HALO + tm:, :] = (norm(xn_ref[...]) * (i < nblk - 1).astype(F32)).astype(BF16)

    def finish_rows(zr_ref, mb, s, r):
        cs = slice(s * FF_SUB, (s + 1) * FF_SUB)

        def conv(half, cw, cb):
            z3 = zr_ref[mb, s, half, r * FF_RB:(r + 1) * FF_RB + 2 * HALO, :].reshape(ng + 2, HALO, FF_SUB)
            down = pltpu.roll(z3, 1, 1)
            up = pltpu.roll(z3, HALO - 1, 1)
            zm1 = jnp.where(sub == 0, down[0:ng], down[1:ng + 1])
            zp1 = jnp.where(sub == HALO - 1, up[2:ng + 2], up[1:ng + 1])
            return (zm1 * cw[0:1] + z3[1:ng + 1] * cw[1:2] + zp1 * cw[2:3] + cb).reshape(FF_RB, FF_SUB)

        a = conv(0, cwa_ref[:, cs], cba_ref[:, cs])
        g = conv(1, cwg_ref[:, cs], cbg_ref[:, cs])
        r0 = mb * FF_MB + r * FF_RB
        out_ref[r0:r0 + FF_RB, cs] = (g * _sigmoid(g) * a).astype(out_ref.dtype)

    def step(zw_ref, zr_ref):
        pieces = [(mb, s, r) for mb in range(n_mb) for s in range(n_sub) for r in range(n_rb)]
        n_mm = n_mb * n_sub * 2 * n_k
        done = m = 0
        for mb in range(n_mb):
            hrows = slice(mb * FF_MB, (mb + 1) * FF_MB + 2 * HALO)
            for s in range(n_sub):
                cs = slice(s * FF_SUB, (s + 1) * FF_SUB)
                for half, w_ref in ((0, wa_ref), (1, wg_ref)):
                    acc = None
                    for kc in range(n_k):
                        ks = slice(kc * FF_KC, (kc + 1) * FF_KC)
                        part = _dot(h_ref[hrows, ks], w_ref[ks, cs])
                        acc = part if acc is None else acc + part
                        m += 1
                        while done * n_mm < m * len(pieces):
                            finish_rows(zr_ref, *pieces[done])
                            done += 1
                    zw_ref[mb, s, half] = acc

    parity = lax.rem(n, 2)
    pl.when(parity == 0)(lambda: step(z0_ref, z1_ref))
    pl.when(parity == 1)(lambda: step(z1_ref, z0_ref))


def _ffn_up(x, norm_w, w_up, conv_w, conv_b, l, tm):
    T, D = x.shape
    n_tiles = pl.cdiv(D_FF, FF_TILE)
    n_steps = (T // tm) * n_tiles
    per = tm // HALO
    last = T // HALO - 1
    El = pl.Element

    def row(n):
        return jnp.minimum(n // n_tiles, T // tm - 1)

    def col(n):
        return jnp.where(n == n_steps, n_tiles - 1, lax.rem(n, n_tiles))

    def prev(n):
        return jnp.maximum(n - 1, 0)

    def off(j, base=0):
        return (base // LANE + jnp.minimum(j * (FF_TILE // LANE), (D_FF - FF_TILE) // LANE)) * LANE

    def halves(rows, which):
        return [pl.BlockSpec((None, El(rows), El(FF_TILE)), lambda n: (l, 0, off(which(n)))),
                pl.BlockSpec((None, El(rows), El(FF_TILE)), lambda n: (l, 0, off(which(n), D_FF)))]

    prev_col = lambda n: lax.rem(prev(n), n_tiles)
    z_shape = (tm // FF_MB, FF_TILE // FF_SUB, 2, FF_MB + 2 * HALO, FF_SUB)
    return pl.pallas_call(
        functools.partial(_ffn_up_kernel, tm=tm, n_tiles=n_tiles, n_steps=n_steps),
        grid=(n_steps + 1,),
        in_specs=[
            pl.BlockSpec((tm, D), lambda n: (row(n), 0)),
            pl.BlockSpec((HALO, D), lambda n: (jnp.maximum(row(n) * per - 1, 0), 0)),
            pl.BlockSpec((HALO, D), lambda n: (jnp.minimum((row(n) + 1) * per, last), 0)),
            pl.BlockSpec((1, D), lambda n: (0, 0)),
        ] + halves(D, col) + halves(3, prev_col) + halves(1, prev_col),
        out_specs=pl.BlockSpec((tm, FF_TILE), lambda n: (prev(n) // n_tiles, prev_col(n))),
        out_shape=jax.ShapeDtypeStruct((T, n_tiles * FF_TILE), BF16),
        scratch_shapes=[pltpu.VMEM((tm + 2 * HALO, D), BF16), pltpu.VMEM(z_shape, F32), pltpu.VMEM(z_shape, F32)],
        compiler_params=_params(("arbitrary",)),
        name="ffn_up",
    )(x, x, x, norm_w.reshape(1, D), w_up, w_up, conv_w, conv_w, conv_b, conv_b)


def _ffn_down_kernel(x_ref, a_ref, w_ref, out_ref):
    full = (D_FF // FF_TILE) * FF_TILE
    acc = _dot(a_ref[:, :full], w_ref[:full, :]) + _dot(a_ref[:, full + FF_DUP:], w_ref[full:, :])
    out_ref[...] = x_ref[...] + acc


def _ffn_down(x, act, w_down, l, tm, tn):
    T, D = x.shape
    K = act.shape[1]
    return pl.pallas_call(
        _ffn_down_kernel,
        grid=(T // tm, D // tn),
        in_specs=[
            pl.BlockSpec((tm, tn), lambda i, j: (i, j)),
            pl.BlockSpec((tm, K), lambda i, j: (i, 0)),
            pl.BlockSpec((None, D_FF, tn), lambda i, j: (l, 0, j)),
        ],
        out_specs=pl.BlockSpec((tm, tn), lambda i, j: (i, j)),
        out_shape=jax.ShapeDtypeStruct((T, D), F32),
        compiler_params=_params(("arbitrary", "arbitrary")),
        name="ffn_down",
    )(x, act, w_down)


def _final_norm_kernel(x_ref, w_ref, out_ref):
    xf = x_ref[...]
    ms = jnp.mean(xf * xf, axis=-1, keepdims=True)
    out_ref[...] = xf * lax.rsqrt(ms + EPS) * w_ref[...]


def _final_norm(x, w, tm):
    T, D = x.shape
    return pl.pallas_call(
        _final_norm_kernel,
        grid=(T // tm,),
        in_specs=[pl.BlockSpec((tm, D), lambda i: (i, 0)), pl.BlockSpec((1, D), lambda i: (0, 0))],
        out_specs=pl.BlockSpec((tm, D), lambda i: (i, 0)),
        out_shape=jax.ShapeDtypeStruct((T, D), F32),
        compiler_params=_params(("arbitrary",)),
        name="final_norm",
    )(x, w.reshape(1, D))


def _split_in_proj(w_in):
    w_main = jnp.concatenate([w_in[..., a:b] for a, b in IN_PROJ_SLICES], axis=-1).astype(BF16)
    n_gate = GATE_SLICE[1] - GATE_SLICE[0]
    zeros = jnp.zeros(w_in.shape[:-1] + (GATE_PAD - n_gate,), BF16)
    w_gate = jnp.concatenate([w_in[..., GATE_SLICE[0]:GATE_SLICE[1]].astype(BF16), zeros], axis=-1)
    return w_main, w_gate


def _tile(T, want):
    return min(want, T)


def kernel(x, norm1_w, w_in, mlstm_conv_w, mlstm_conv_b, mlstm_i_bias, mlstm_f_bias, mlstm_norm_w,
           hgrn_lb_logits, hgrn_f_bias, hgrn_norm_w, sgu_ln_w, sgu_ln_b, sgu_w, sgu_b, na_rel_bias,
           out_norm_w, w_out, norm2_w, ffn_w_up, ffn_conv_w, ffn_conv_b, ffn_w_down, final_norm_w):
    B, T, D = x.shape
    assert B == 1 and D == D_MODEL and T % (NA_ROWS * GRID_W) == 0
    depth = w_in.shape[0]
    tm = _tile(T, 1024)
    G = _tile(T, 512)

    lb_all = jax.nn.softmax(hgrn_lb_logits.astype(F32), axis=0)
    lower_bounds = jnp.cumsum(lb_all, axis=0) - lb_all[0]

    w_main, w_gate = _split_in_proj(w_in)
    w_out_b = w_out.astype(BF16)
    w_up = ffn_w_up.astype(BF16)
    conv_w = ffn_conv_w.astype(F32)
    conv_b = ffn_conv_b.astype(F32)[:, None, :]
    w_dn = ffn_w_down.astype(BF16)
    bands = _na_bias_bands(na_rel_bias)

    xs = x[0]
    for l in range(depth):
        proj, proj_b, gates = _proj_in(xs, norm1_w[l], w_main, w_gate, l, tm)
        y_a = _mlstm(proj, proj_b, gates, mlstm_conv_w[l], mlstm_conv_b[l], mlstm_i_bias[l], mlstm_f_bias[l],
                     mlstm_norm_w[l], G)
        y_b = _hgrn(proj, proj_b, lower_bounds[l], hgrn_f_bias[l], hgrn_norm_w[l], G)
        y_c = _sgu(proj, sgu_ln_w[l], sgu_ln_b[l], sgu_w[l], sgu_b[l], out_norm_w[l, :W_GRP], G)
        y_d = _na(proj_b, bands, out_norm_w[l, W_GRP:], l, NA_ROWS)
        xs = _proj_out(xs, (y_a, y_b, y_c, y_d), w_out_b, l, tm, 1024)
        act = _ffn_up(xs, norm2_w[l], w_up, conv_w, conv_b, l, tm)
        xs = _ffn_down(xs, act, w_dn, l, tm, 512)
    return _final_norm(xs, final_norm_w, tm)[None]
```

```python
import functools

import numpy as np
import jax
import jax.numpy as jnp
from jax import lax
from jax.experimental import pallas as pl
from jax.experimental.pallas import tpu as pltpu

F32 = jnp.float32
BF16 = jnp.bfloat16
HIGHEST = lax.Precision.HIGHEST

D_MODEL = 2048
HEAD_DIM = 128
N_HEADS = 4
W_GRP = N_HEADS * HEAD_DIM
D_FF = 5504
GRID_W = 64
NA_ROWS = 8
NA_COLS = 16
NA_ROW_BLOCK = 32
EPS = 1e-6
NEG_BIG = -1e30

LIN_CHUNK = 128
SGU_CHUNK = 128
GATE_PAD = 128
FF_TILE = 512
FF_SUB = 256
FF_DUP = -D_FF % FF_TILE
FF_MB = 512
FF_KC = 256
FF_RB = 32
HALO = 8
LANE = 128

COL_A_Q, COL_A_K = 0, 1
COL_B_Q, COL_B_FF, COL_B_FB, COL_B_G = 2, 3, 4, 5
COL_C_U, COL_C_V = 6, 7
N_F32 = 8
COL_A_V, COL_A_O, COL_B_I, COL_D_Q, COL_D_K, COL_D_V = 0, 1, 2, 3, 4, 5
N_BF16 = 6
IN_PROJ_SLICES = ((0, 1024), (2064, 3600), (4112, 4624), (4624, 5648),
                  (1024, 2048), (3600, 4112), (5648, 7184))
PROJ_TILE = 2 * W_GRP
GATE_SLICE = (2048, 2064)

VMEM_LIMIT = 56 * 1024 * 1024

_NT = (((1,), (1,)), ((), ()))
_TN = (((0,), (0,)), ((), ()))


def _dot(a, b, dims=None, precision=None):
    if dims is None:
        return jnp.dot(a, b, preferred_element_type=F32, precision=precision)
    return lax.dot_general(a, b, dims, preferred_element_type=F32, precision=precision)


def _bdot(a, b, dims=None):
    return _dot(a.astype(BF16), b.astype(BF16), dims)


def _sigmoid(x):
    return jax.nn.sigmoid(x)


def _log_sigmoid(x):
    return jnp.minimum(x, 0.0) - jnp.log1p(jnp.exp(-jnp.abs(x)))


def _neg_abs(x):
    return -jnp.abs(x)


def _head_mean(x):
    hi = x.astype(BF16)
    lo = (x - hi.astype(F32)).astype(BF16)
    avg = jnp.full((HEAD_DIM, HEAD_DIM), 1.0 / HEAD_DIM, BF16)
    return _dot(hi, avg) + _dot(lo, avg)


def _params(sem):
    return pltpu.CompilerParams(dimension_semantics=sem, vmem_limit_bytes=VMEM_LIMIT)


def _proj_in_kernel(x_ref, nw_ref, w_ref, wg_ref, pf_ref, pb_ref, gates_ref, h_ref):
    j = pl.program_id(1)

    @pl.when(j == 0)
    def _():
        xf = x_ref[...]
        ms = jnp.mean(xf * xf, axis=-1, keepdims=True)
        h = (xf * lax.rsqrt(ms + EPS) * nw_ref[...]).astype(BF16)
        h_ref[...] = h
        gates_ref[...] = _dot(h, wg_ref[...])

    n_f32_tiles = N_F32 * W_GRP // PROJ_TILE

    @pl.when(j < n_f32_tiles)
    def _():
        pf_ref[...] = _dot(h_ref[...], w_ref[...])

    @pl.when(j >= n_f32_tiles)
    def _():
        pb_ref[...] = _dot(h_ref[...], w_ref[...]).astype(BF16)


def _proj_in(x, norm_w, w_main, w_gate, l, tm):
    T, D = x.shape
    nf = N_F32 * W_GRP // PROJ_TILE
    nb = N_BF16 * W_GRP // PROJ_TILE
    return pl.pallas_call(
        _proj_in_kernel,
        grid=(T // tm, nf + nb),
        in_specs=[
            pl.BlockSpec((tm, D), lambda i, j: (i, 0)),
            pl.BlockSpec((1, D), lambda i, j: (0, 0)),
            pl.BlockSpec((None, D, PROJ_TILE), lambda i, j: (l, 0, j)),
            pl.BlockSpec((None, D, GATE_PAD), lambda i, j: (l, 0, 0)),
        ],
        out_specs=[
            pl.BlockSpec((tm, PROJ_TILE), lambda i, j: (i, jnp.minimum(j, nf - 1))),
            pl.BlockSpec((tm, PROJ_TILE), lambda i, j: (i, jnp.maximum(j - nf, 0))),
            pl.BlockSpec((tm, GATE_PAD), lambda i, j: (i, 0)),
        ],
        out_shape=[
            jax.ShapeDtypeStruct((T, N_F32 * W_GRP), F32),
            jax.ShapeDtypeStruct((T, N_BF16 * W_GRP), BF16),
            jax.ShapeDtypeStruct((T, GATE_PAD), F32),
        ],
        scratch_shapes=[pltpu.VMEM((tm, D), BF16)],
        compiler_params=_params(("arbitrary", "arbitrary")),
        name="proj_in",
    )(x, norm_w.reshape(1, D), w_main, w_gate)


def _causal_mask(L, reverse):
    t = np.arange(L)[:, None]
    s = np.arange(L)[None, :]
    return ((s >= t) if reverse else (s <= t)).astype(np.float32)


def _shift_rows(x, prev_grp, next_grp):
    n, w = x.shape
    ng = n // HALO
    sub = lax.broadcasted_iota(jnp.int32, (1, HALO, 1), 1)
    x3 = jnp.concatenate([prev_grp, x, next_grp], axis=0).reshape(ng + 2, HALO, w)
    down = pltpu.roll(x3, 1, 1)
    up = pltpu.roll(x3, HALO - 1, 1)
    xm1 = jnp.where(sub == 0, down[0:ng], down[1:ng + 1])
    xp1 = jnp.where(sub == HALO - 1, up[2:ng + 2], up[1:ng + 1])
    return xm1.reshape(n, w), xp1.reshape(n, w)


def _dwconv3(x, prev_grp, next_grp, w3, b):
    xm1, xp1 = _shift_rows(x, prev_grp, next_grp)
    return xm1 * w3[0:1] + x * w3[1:2] + xp1 * w3[2:3] + b


def _halo_specs(G, T, width, col, rev):
    nb = T // G
    per = G // HALO
    last = T // HALO - 1

    def blk(c):
        return (nb - 1 - c) if rev else c

    prev = pl.BlockSpec((HALO, width), lambda c: (jnp.maximum(blk(c) * per - 1, 0), col))
    nxt = pl.BlockSpec((HALO, width), lambda c: (jnp.minimum((blk(c) + 1) * per, last), col))
    return prev, nxt


def _mlstm_kernel(*refs, reverse, G, L):
    if reverse:
        (q_ref, qp_ref, qn_ref, k_ref, kp_ref, kn_ref, v_ref, g_ref, gt_ref, gb_ref, gbt_ref,
         cw_ref, cb_ref, mask_ref, hf_ref, o_ref, nw_ref, out_ref,
         qc_ref, kc_ref, ct_ref, m_ref) = refs
    else:
        (q_ref, qp_ref, qn_ref, k_ref, kp_ref, kn_ref, v_ref, g_ref, gt_ref, gb_ref, gbt_ref,
         cw_ref, cb_ref, mask_ref, out_ref,
         qc_ref, kc_ref, ct_ref, m_ref) = refs
    c = pl.program_id(0)
    nblk = pl.num_programs(0)
    blk = (nblk - 1 - c) if reverse else c

    @pl.when(c == 0)
    def _():
        ct_ref[...] = jnp.zeros_like(ct_ref)
        m_ref[...] = jnp.zeros_like(m_ref)

    has_prev = (blk > 0).astype(F32)
    has_next = (blk < nblk - 1).astype(F32)
    cw = cw_ref[...]
    cb = cb_ref[...]
    zq = _dwconv3(q_ref[...], qp_ref[...] * has_prev, qn_ref[...] * has_next, cw[:, :W_GRP], cb[:, :W_GRP])
    qc_ref[...] = zq * _sigmoid(zq)
    zk = _dwconv3(k_ref[...], kp_ref[...] * has_prev, kn_ref[...] * has_next, cw[:, W_GRP:], cb[:, W_GRP:])
    kc_ref[...] = zk * _sigmoid(zk) * (HEAD_DIM ** -0.5)

    maskf = mask_ref[...]
    mask = maskf > 0.5
    ones_col = jnp.ones((L, HEAD_DIM), BF16)
    d = 1 if reverse else 0
    nsub = G // L

    def chunk(j, carry):
        jj = (nsub - 1 - j) if reverse else j
        r0 = pl.multiple_of(jj * L, L)
        rows = pl.ds(r0, L)
        gch = g_ref[rows, :] + gb_ref[...]
        b_cols = _dot(maskf, _log_sigmoid(gch), precision=HIGHEST)
        gt = gt_ref[jj] + gbt_ref[...]
        b_rows = _dot(_log_sigmoid(gt), maskf, _NT, precision=HIGHEST)
        heads = range(N_HEADS)
        hcs = [slice(h * HEAD_DIM, (h + 1) * HEAD_DIM) for h in heads]
        qh = [qc_ref[rows, hc].astype(BF16) for hc in hcs]
        kh = [kc_ref[rows, hc].astype(BF16) for hc in hcs]
        v_aug = [jnp.concatenate([v_ref[rows, hc], ones_col], axis=1) for hc in hcs]
        ct = [ct_ref[h] for h in heads]
        m_prev = [m_ref[h][0:1, 0:1] for h in heads]
        qk = [_dot(qh[h], kh[h], _NT) for h in heads]
        qc = [_dot(qh[h], ct[h].astype(BF16)) for h in heads]
        w_intra, w_inter, m_t, decay, upd = [], [], [], [], []
        for h in heads:
            ci = d * N_HEADS + h
            cf = 2 * N_HEADS + ci
            i_col, b_col = gch[:, ci:ci + 1], b_cols[:, cf:cf + 1]
            i_row, b_row = gt[ci:ci + 1, :], b_rows[cf:cf + 1, :]
            dmat = jnp.where(mask, b_col - b_row + i_row, NEG_BIG)
            inter = b_col + m_prev[h]
            m_t.append(jnp.maximum(inter, jnp.max(dmat, axis=1, keepdims=True)))
            w_intra.append(jnp.where(mask, jnp.exp(dmat - m_t[h]), 0.0))
            w_inter.append(jnp.exp(inter - m_t[h]))
            b_last = b_col[0:1] if reverse else b_col[L - 1:L]
            a_col = i_col + b_last - b_col
            m_new = jnp.maximum(b_last + m_prev[h], jnp.max(a_col, axis=0, keepdims=True))
            decay.append(jnp.exp(b_last + m_prev[h] - m_new))
            wk = jnp.exp(a_col - m_new)
            upd.append(_bdot(kh[h], wk * v_aug[h], _TN))
            m_ref[h] = jnp.broadcast_to(m_new, m_ref.shape[1:])
        sv = [_bdot(qk[h] * w_intra[h], v_aug[h]) for h in heads]
        hh = []
        for h in heads:
            ct_ref[h] = decay[h] * ct[h] + upd[h]
            num = sv[h] + w_inter[h] * qc[h]
            den = num[:, HEAD_DIM:]
            hh.append(num[:, :HEAD_DIM] / jnp.maximum(jnp.abs(den), jnp.exp(-m_t[h])))
        if reverse:
            tot = [hf_ref[rows, hcs[h]] + hh[h] for h in heads]
            mu = [_head_mean(tot[h]) for h in heads]
            tc = [tot[h] - mu[h] for h in heads]
            var = [_head_mean(tc[h] * tc[h]) for h in heads]
            for h in heads:
                y = tc[h] * lax.rsqrt(var[h] + EPS) * nw_ref[:, hcs[h]] * _sigmoid(o_ref[rows, hcs[h]].astype(F32))
                out_ref[rows, hcs[h]] = y.astype(out_ref.dtype)
        else:
            for h in heads:
                out_ref[rows, hcs[h]] = hh[h]
        return carry

    lax.fori_loop(0, nsub, chunk, 0, unroll=True)


def _mlstm_dir(proj, proj_b, gates, gates_t, gb, gbt, conv_w, conv_b, reverse, G, hf=None, norm_w=None):
    T = proj.shape[0]
    L = LIN_CHUNK
    nb = T // G

    def blk(c):
        return (nb - 1 - c) if reverse else c

    def col_spec(col):
        return pl.BlockSpec((G, W_GRP), lambda c: (blk(c), col))

    qp, qn = _halo_specs(G, T, W_GRP, COL_A_Q, reverse)
    kp, kn = _halo_specs(G, T, W_GRP, COL_A_K, reverse)
    const2 = lambda shape: pl.BlockSpec(shape, lambda c: (0, 0))
    in_specs = [
        col_spec(COL_A_Q), qp, qn, col_spec(COL_A_K), kp, kn, col_spec(COL_A_V),
        pl.BlockSpec((G, GATE_PAD), lambda c: (blk(c), 0)),
        pl.BlockSpec((G // L, 4 * N_HEADS, L), lambda c: (blk(c), 0, 0)),
        const2((1, GATE_PAD)), const2((4 * N_HEADS, 1)),
        const2((3, 2 * W_GRP)), const2((1, 2 * W_GRP)), const2((L, L)),
    ]
    args = [proj, proj, proj, proj, proj, proj, proj_b, gates, gates_t, gb, gbt,
            conv_w, conv_b.reshape(1, -1), jnp.asarray(_causal_mask(L, reverse))]
    if reverse:
        in_specs += [pl.BlockSpec((G, W_GRP), lambda c: (blk(c), 0)), col_spec(COL_A_O),
                     const2((1, W_GRP))]
        args += [hf, proj_b, norm_w.reshape(1, -1)]
    return pl.pallas_call(
        functools.partial(_mlstm_kernel, reverse=reverse, G=G, L=L),
        grid=(nb,),
        in_specs=in_specs,
        out_specs=pl.BlockSpec((G, W_GRP), lambda c: (blk(c), 0)),
        out_shape=jax.ShapeDtypeStruct((T, W_GRP), BF16 if reverse else F32),
        scratch_shapes=[
            pltpu.VMEM((G, W_GRP), F32), pltpu.VMEM((G, W_GRP), F32),
            pltpu.VMEM((N_HEADS, HEAD_DIM, 2 * HEAD_DIM), F32),
            pltpu.VMEM((N_HEADS, 8, 128), F32),
        ],
        compiler_params=_params(("arbitrary",)),
        name="mlstm_bwd" if reverse else "mlstm_fwd",
    )(*args)


def _mlstm(proj, proj_b, gates, conv_w, conv_b, i_bias, f_bias, norm_w, G):
    T = proj.shape[0]
    L = LIN_CHUNK
    ng = 4 * N_HEADS
    gbias = jnp.concatenate([i_bias.reshape(-1), f_bias.reshape(-1)]).astype(F32)
    gb = jnp.pad(gbias, (0, GATE_PAD - ng)).reshape(1, GATE_PAD)
    gbt = gbias.reshape(ng, 1)
    gates_t = gates[:, :ng].reshape(T // L, L, ng).transpose(0, 2, 1)
    hf = _mlstm_dir(proj, proj_b, gates, gates_t, gb, gbt, conv_w, conv_b, False, G)
    return _mlstm_dir(proj, proj_b, gates, gates_t, gb, gbt, conv_w, conv_b, True, G, hf, norm_w)


def _hgrn_level_masks(L, reverse):
    n_lev = int(np.log2(L))
    t = np.arange(L)
    lmask = []
    for lev in range(n_lev):
        h = 1 << lev
        blk = t // (2 * h)
        upper = (t // h) % 2 == 1
        is_q = ~upper if reverse else upper
        same = blk[:, None] == blk[None, :]
        lmask.append((same & is_q[:, None] & (~is_q)[None, :]).astype(np.float32))
    lmask.append(np.eye(L, dtype=np.float32))
    return np.stack(lmask)


def _anchor_rows(b, lev, reverse):
    L, W = b.shape
    h = 1 << lev
    a = h if reverse else h - 1
    if 2 * h >= 8:
        b3 = b.reshape(L // (2 * h), 2 * h, W)
        return jnp.broadcast_to(b3[:, a:a + 1, :], b3.shape).reshape(L, W)
    b3 = b.reshape(L // 8, 8, W)
    sub = lax.broadcasted_iota(jnp.int32, (1, 8, 1), 1)
    out = None
    for g in range(8 // (2 * h)):
        piece = jnp.broadcast_to(b3[:, g * 2 * h + a:g * 2 * h + a + 1, :], b3.shape)
        out = piece if out is None else jnp.where(sub >= g * 2 * h, piece, out)
    return out.reshape(L, W)


def _hgrn_kernel(*refs, reverse, G, L):
    if reverse:
        (q_ref, f_ref, i_ref, lb_ref, fb_ref, mask_ref, lmask_ref, of_ref, g_ref, nw_ref,
         out_ref, st_ref) = refs
    else:
        (q_ref, f_ref, i_ref, lb_ref, fb_ref, mask_ref, lmask_ref, out_ref, st_ref) = refs
    c = pl.program_id(0)

    @pl.when(c == 0)
    def _():
        st_ref[...] = jnp.zeros_like(st_ref)

    n_lev = lmask_ref.shape[0] - 1
    nsub = G // L
    heads = range(N_HEADS)
    hcs = [slice(h * HEAD_DIM, (h + 1) * HEAD_DIM) for h in heads]
    t_idx = lax.broadcasted_iota(jnp.int32, (L, 1), 0)

    def chunk(j, carry):
        jj = (nsub - 1 - j) if reverse else j
        r0 = pl.multiple_of(jj * L, L)
        rows = pl.ds(r0, L)
        lb = lb_ref[...]
        z = f_ref[rows, :] + fb_ref[...]
        sig = _sigmoid(z)
        logf = jnp.log2(lb + (1.0 - lb) * sig)
        key = (1.0 - lb) * (1.0 - sig)
        qs = q_ref[rows, :]
        qs = qs * _sigmoid(qs)
        b = _dot(mask_ref[...], logf, precision=HIGHEST)
        b_last = b[0:1] if reverse else b[L - 1:L]
        vh = [i_ref[rows, hc].astype(BF16) for hc in hcs]
        st = [st_ref[h] for h in heads]
        q_in = (qs * jnp.exp2(b)).astype(BF16)
        k_out = (key * jnp.exp2(b_last - b)).astype(BF16)
        inter = [_dot(q_in[:, hcs[h]], st[h].astype(BF16), _NT) for h in heads]
        upd = [_dot(vh[h], k_out[:, hcs[h]], _TN) for h in heads]
        qb, kb = qs.astype(BF16), key.astype(BF16)
        scores = [lmask_ref[n_lev] * _dot(qb[:, hcs[h]], kb[:, hcs[h]], _NT) for h in heads]
        for lev in range(n_lev):
            is_q = ((t_idx >> lev) & 1) == (0 if reverse else 1)
            x = (jnp.where(is_q, qs, key) * jnp.exp2(_neg_abs(b - _anchor_rows(b, lev, reverse)))).astype(BF16)
            for h in heads:
                scores[h] = scores[h] + lmask_ref[lev] * _dot(x[:, hcs[h]], x[:, hcs[h]], _NT)
        decay = jnp.exp2(b_last)
        o = []
        for h in heads:
            st_ref[h] = st[h] * decay[:, hcs[h]] + upd[h]
            o.append(_dot(scores[h].astype(BF16), vh[h]) + inter[h])
        if reverse:
            tot = [of_ref[rows, hcs[h]] + o[h] for h in heads]
            for h in heads:
                gg = g_ref[rows, hcs[h]]
                ms = jnp.mean(tot[h] * tot[h], axis=-1, keepdims=True)
                y = tot[h] * lax.rsqrt(ms + EPS) * nw_ref[:, hcs[h]] * (gg * _sigmoid(gg))
                out_ref[rows, hcs[h]] = y.astype(out_ref.dtype)
        else:
            for h in heads:
                out_ref[rows, hcs[h]] = o[h]
        return carry

    lax.fori_loop(0, nsub, chunk, 0, unroll=True)


def _hgrn_dir(proj, proj_b, lb, fb, reverse, G, of=None, norm_w=None):
    T = proj.shape[0]
    L = LIN_CHUNK
    nb = T // G
    mask = _causal_mask(L, reverse)
    lmask = _hgrn_level_masks(L, reverse)

    def blk(c):
        return (nb - 1 - c) if reverse else c

    def col_spec(col):
        return pl.BlockSpec((G, W_GRP), lambda c: (blk(c), col))

    const2 = lambda shape: pl.BlockSpec(shape, lambda c: (0, 0))
    const3 = lambda shape: pl.BlockSpec(shape, lambda c: (0, 0, 0))
    in_specs = [col_spec(COL_B_Q), col_spec(COL_B_FB if reverse else COL_B_FF), col_spec(COL_B_I),
                const2((1, W_GRP)), const2((1, W_GRP)), const2(mask.shape), const3(lmask.shape)]
    args = [proj, proj, proj_b, lb.reshape(1, -1), fb.reshape(1, -1), jnp.asarray(mask), jnp.asarray(lmask)]
    if reverse:
        in_specs += [pl.BlockSpec((G, W_GRP), lambda c: (blk(c), 0)), col_spec(COL_B_G), const2((1, W_GRP))]
        args += [of, proj, norm_w.reshape(1, -1)]
    return pl.pallas_call(
        functools.partial(_hgrn_kernel, reverse=reverse, G=G, L=L),
        grid=(nb,),
        in_specs=in_specs,
        out_specs=pl.BlockSpec((G, W_GRP), lambda c: (blk(c), 0)),
        out_shape=jax.ShapeDtypeStruct((T, W_GRP), BF16 if reverse else F32),
        scratch_shapes=[pltpu.VMEM((N_HEADS, HEAD_DIM, HEAD_DIM), F32)],
        compiler_params=_params(("arbitrary",)),
        name="hgrn_bwd" if reverse else "hgrn_fwd",
    )(*args)


def _hgrn(proj, proj_b, lb, f_bias, norm_w, G):
    of = _hgrn_dir(proj, proj_b, lb, f_bias[0], False, G)
    return _hgrn_dir(proj, proj_b, lb, f_bias[1], True, G, of, norm_w)


def _gelu(x):
    return 0.5 * x * (1.0 + lax.erf(x * (2.0 ** -0.5)))


def _sgu_kernel(u_ref, v_ref, lnw_ref, lnb_ref, ws_ref, bs_ref, nw_ref, out_ref, *, G):
    v = _gelu(v_ref[...])
    mu = jnp.mean(v, axis=-1, keepdims=True)
    vc = v - mu
    var = jnp.mean(vc * vc, axis=-1, keepdims=True)
    vn = (vc * lax.rsqrt(var + EPS) * lnw_ref[...] + lnb_ref[...]).astype(BF16)
    for n in range(G // SGU_CHUNK):
        rows = slice(n * SGU_CHUNK, (n + 1) * SGU_CHUNK)
        for g in range(N_HEADS):
            hc = slice(g * HEAD_DIM, (g + 1) * HEAD_DIM)
            mixed = _dot(ws_ref[g], vn[rows, hc]) + bs_ref[:, g:g + 1]
            y = _gelu(u_ref[rows, hc]) * mixed
            ms = jnp.mean(y * y, axis=-1, keepdims=True)
            out_ref[rows, hc] = (y * lax.rsqrt(ms + EPS) * nw_ref[:, hc]).astype(out_ref.dtype)


def _sgu(proj, ln_w, ln_b, w_s, b_s, norm_w, G):
    T = proj.shape[0]
    row = lambda a: a.reshape(1, -1)
    const2 = lambda shape: pl.BlockSpec(shape, lambda c: (0, 0))
    return pl.pallas_call(
        functools.partial(_sgu_kernel, G=G),
        grid=(T // G,),
        in_specs=[
            pl.BlockSpec((G, W_GRP), lambda c: (c, COL_C_U)),
            pl.BlockSpec((G, W_GRP), lambda c: (c, COL_C_V)),
            const2((1, W_GRP)), const2((1, W_GRP)),
            pl.BlockSpec((N_HEADS, SGU_CHUNK, SGU_CHUNK), lambda c: (0, 0, 0)),
            const2((SGU_CHUNK, N_HEADS)), const2((1, W_GRP)),
        ],
        out_specs=pl.BlockSpec((G, W_GRP), lambda c: (c, 0)),
        out_shape=jax.ShapeDtypeStruct((T, W_GRP), BF16),
        compiler_params=_params(("arbitrary",)),
        name="sgu",
    )(proj, proj, row(ln_w), row(ln_b), w_s.astype(BF16), b_s.T, row(norm_w))


def _na_bias_bands(rel_bias):
    cols = np.arange(GRID_W)
    col_start = np.clip(cols - NA_COLS // 2, 0, GRID_W - NA_COLS)
    cc = cols[None, :]
    valid = (cc >= col_start[:, None]) & (cc < col_start[:, None] + NA_COLS)
    col_off = cc - cols[:, None] + (NA_COLS - 1)
    n_off = 2 * NA_COLS - 1
    onehot = ((col_off[None] == np.arange(n_off)[:, None, None]) & valid[None]).astype(np.float32)
    lead = rel_bias.shape[:-1]
    bands = jnp.dot(rel_bias.astype(F32).reshape(-1, n_off), jnp.asarray(onehot.reshape(n_off, -1)),
                    precision=HIGHEST)
    bands = bands + jnp.asarray(np.where(valid, 0.0, NEG_BIG).astype(np.float32).reshape(1, -1))
    return bands.reshape(*lead, GRID_W, GRID_W)


def _na_kernel(q_ref, k_ref, v_ref, bands_ref, nw_ref, out_ref, bias_ref, *, RB, n_rows):
    R = pl.program_id(1)
    win = NA_ROWS * GRID_W
    scale = HEAD_DIM ** -0.5

    @pl.when(R == 0)
    def _():
        for e in range(NA_ROWS):
            for j in range(NA_ROWS):
                bias_ref[e, :, j * GRID_W:(j + 1) * GRID_W] = bands_ref[j - e + NA_ROWS - 1]

    krows, es = [], []
    for i in range(RB):
        r = R * RB + i
        rs = jnp.clip(r - NA_ROWS // 2, 0, n_rows - NA_ROWS)
        es.append(r - rs)
        krows.append(pl.ds(pl.multiple_of(rs * GRID_W, GRID_W), win))
    qrows = [slice(i * GRID_W, (i + 1) * GRID_W) for i in range(RB)]
    ss = [_bdot(q_ref[qrows[i], :], k_ref[krows[i], :], _NT) for i in range(RB)]
    ps, sums = [], []
    for i in range(RB):
        s = ss[i] * scale + bias_ref[es[i]]
        p = jnp.exp(s - jnp.max(s, axis=-1, keepdims=True))
        sums.append(jnp.sum(p, axis=-1, keepdims=True))
        ps.append(p.astype(BF16))
    os_ = [_dot(ps[i], v_ref[krows[i], :].astype(BF16)) for i in range(RB)]
    for i in range(RB):
        o = os_[i] / sums[i]
        ms = jnp.mean(o * o, axis=-1, keepdims=True)
        out_ref[qrows[i], :] = (o * lax.rsqrt(ms + EPS) * nw_ref[...]).astype(out_ref.dtype)


def _na(proj_b, bands, norm_w, l, RB):
    T = proj_b.shape[0]
    n_rows = T // GRID_W
    cq, ck, cv = (c * N_HEADS for c in (COL_D_Q, COL_D_K, COL_D_V))
    return pl.pallas_call(
        functools.partial(_na_kernel, RB=RB, n_rows=n_rows),
        grid=(N_HEADS, n_rows // RB),
        in_specs=[
            pl.BlockSpec((RB * GRID_W, HEAD_DIM), lambda h, r: (r, cq + h)),
            pl.BlockSpec((T, HEAD_DIM), lambda h, r: (0, ck + h)),
            pl.BlockSpec((T, HEAD_DIM), lambda h, r: (0, cv + h)),
            pl.BlockSpec((None, None, 2 * NA_ROWS - 1, GRID_W, GRID_W), lambda h, r: (l, h, 0, 0, 0)),
            pl.BlockSpec((1, HEAD_DIM), lambda h, r: (0, h)),
        ],
        out_specs=pl.BlockSpec((RB * GRID_W, HEAD_DIM), lambda h, r: (r, h)),
        out_shape=jax.ShapeDtypeStruct((T, W_GRP), BF16),
        scratch_shapes=[pltpu.VMEM((NA_ROWS, GRID_W, NA_ROWS * GRID_W), F32)],
        compiler_params=_params(("arbitrary", "arbitrary")),
        name="na",
    )(proj_b, proj_b, proj_b, bands, norm_w.reshape(1, -1))


def _proj_out_kernel(x_ref, ya_ref, yb_ref, yc_ref, yd_ref, wa_ref, wb_ref, wc_ref, wd_ref, out_ref):
    acc = x_ref[...]
    for y_ref, w_ref in ((ya_ref, wa_ref), (yb_ref, wb_ref), (yc_ref, wc_ref), (yd_ref, wd_ref)):
        acc = acc + _dot(y_ref[...], w_ref[...])
    out_ref[...] = acc


def _proj_out(x, ys, w_out, l, tm, tn):
    T, D = x.shape
    y_spec = pl.BlockSpec((tm, W_GRP), lambda i, j: (i, 0))
    w_specs = [pl.BlockSpec((None, W_GRP, tn), functools.partial(lambda i, j, g: (l, g, j), g=g)) for g in range(4)]
    return pl.pallas_call(
        _proj_out_kernel,
        grid=(T // tm, D // tn),
        in_specs=[pl.BlockSpec((tm, tn), lambda i, j: (i, j))] + [y_spec] * 4 + w_specs,
        out_specs=pl.BlockSpec((tm, tn), lambda i, j: (i, j)),
        out_shape=jax.ShapeDtypeStruct((T, D), F32),
        compiler_params=_params(("arbitrary", "arbitrary")),
        name="proj_out",
    )(x, *ys, w_out, w_out, w_out, w_out)


def _ffn_up_kernel(x_ref, xp_ref, xn_ref, nw_ref, wa_ref, wg_ref, cwa_ref, cwg_ref, cba_ref, cbg_ref,
                   out_ref, h_ref, z0_ref, z1_ref, *, tm, n_tiles, n_steps):
    n = pl.program_id(0)
    nblk = n_steps // n_tiles
    i = jnp.minimum(n // n_tiles, nblk - 1)
    d_model = h_ref.shape[1]
    n_mb = tm // FF_MB
    n_sub = FF_TILE // FF_SUB
    n_k = d_model // FF_KC
    n_rb = FF_MB // FF_RB
    ng = FF_RB // HALO
    sub = lax.broadcasted_iota(jnp.int32, (1, HALO, 1), 1)

    def norm(xf):
        ms = jnp.mean(xf * xf, axis=-1, keepdims=True)
        return xf * lax.rsqrt(ms + EPS) * nw_ref[...]

    @pl.when(n == 0)
    def _():
        z1_ref[...] = jnp.zeros_like(z1_ref)

    @pl.when((lax.rem(n, n_tiles) == 0) & (n < n_steps))
    def _():
        h_ref[0:HALO, :] = (norm(xp_ref[...]) * (i > 0).astype(F32)).astype(BF16)
        h_ref[HALO:HALO + tm, :] = norm(x_ref[...]).astype(BF16)
        h_ref[HALO + tm:, :] = (norm(xn_ref[...]) * (i < nblk - 1).astype(F32)).astype(BF16)

    def finish_rows(zr_ref, mb, s, r):
        cs = slice(s * FF_SUB, (s + 1) * FF_SUB)

        def conv(half, cw, cb):
            z3 = zr_ref[mb, s, half, r * FF_RB:(r + 1) * FF_RB + 2 * HALO, :].reshape(ng + 2, HALO, FF_SUB)
            down = pltpu.roll(z3, 1, 1)
            up = pltpu.roll(z3, HALO - 1, 1)
            zm1 = jnp.where(sub == 0, down[0:ng], down[1:ng + 1])
            zp1 = jnp.where(sub == HALO - 1, up[2:ng + 2], up[1:ng + 1])
            return (zm1 * cw[0:1] + z3[1:ng + 1] * cw[1:2] + zp1 * cw[2:3] + cb).reshape(FF_RB, FF_SUB)

        a = conv(0, cwa_ref[:, cs], cba_ref[:, cs])
        g = conv(1, cwg_ref[:, cs], cbg_ref[:, cs])
        r0 = mb * FF_MB + r * FF_RB
        out_ref[r0:r0 + FF_RB, cs] = (g * _sigmoid(g) * a).astype(out_ref.dtype)

    def step(zw_ref, zr_ref):
        pieces = [(mb, s, r) for mb in range(n_mb) for s in range(n_sub) for r in range(n_rb)]
        n_mm = n_mb * n_sub * 2 * n_k
        done = m = 0
        for mb in range(n_mb):
            hrows = slice(mb * FF_MB, (mb + 1) * FF_MB + 2 * HALO)
            for s in range(n_sub):
                cs = slice(s * FF_SUB, (s + 1) * FF_SUB)
                for half, w_ref in ((0, wa_ref), (1, wg_ref)):
                    acc = None
                    for kc in range(n_k):
                        ks = slice(kc * FF_KC, (kc + 1) * FF_KC)
                        part = _dot(h_ref[hrows, ks], w_ref[ks, cs])
                        acc = part if acc is None else acc + part
                        m += 1
                        while done * n_mm < m * len(pieces):
                            finish_rows(zr_ref, *pieces[done])
                            done += 1
                    zw_ref[mb, s, half] = acc

    parity = lax.rem(n, 2)
    pl.when(parity == 0)(lambda: step(z0_ref, z1_ref))
    pl.when(parity == 1)(lambda: step(z1_ref, z0_ref))


def _ffn_up(x, norm_w, w_up, conv_w, conv_b, l, tm):
    T, D = x.shape
    n_tiles = pl.cdiv(D_FF, FF_TILE)
    n_steps = (T // tm) * n_tiles
    per = tm // HALO
    last = T // HALO - 1
    El = pl.Element

    def row(n):
        return jnp.minimum(n // n_tiles, T // tm - 1)

    def col(n):
        return jnp.where(n == n_steps, n_tiles - 1, lax.rem(n, n_tiles))

    def prev(n):
        return jnp.maximum(n - 1, 0)

    def off(j, base=0):
        return (base // LANE + jnp.minimum(j * (FF_TILE // LANE), (D_FF - FF_TILE) // LANE)) * LANE

    def halves(rows, which):
        return [pl.BlockSpec((None, El(rows), El(FF_TILE)), lambda n: (l, 0, off(which(n)))),
                pl.BlockSpec((None, El(rows), El(FF_TILE)), lambda n: (l, 0, off(which(n), D_FF)))]

    prev_col = lambda n: lax.rem(prev(n), n_tiles)
    z_shape = (tm // FF_MB, FF_TILE // FF_SUB, 2, FF_MB + 2 * HALO, FF_SUB)
    return pl.pallas_call(
        functools.partial(_ffn_up_kernel, tm=tm, n_tiles=n_tiles, n_steps=n_steps),
        grid=(n_steps + 1,),
        in_specs=[
            pl.BlockSpec((tm, D), lambda n: (row(n), 0)),
            pl.BlockSpec((HALO, D), lambda n: (jnp.maximum(row(n) * per - 1, 0), 0)),
            pl.BlockSpec((HALO, D), lambda n: (jnp.minimum((row(n) + 1) * per, last), 0)),
            pl.BlockSpec((1, D), lambda n: (0, 0)),
        ] + halves(D, col) + halves(3, prev_col) + halves(1, prev_col),
        out_specs=pl.BlockSpec((tm, FF_TILE), lambda n: (prev(n) // n_tiles, prev_col(n))),
        out_shape=jax.ShapeDtypeStruct((T, n_tiles * FF_TILE), BF16),
        scratch_shapes=[pltpu.VMEM((tm + 2 * HALO, D), BF16), pltpu.VMEM(z_shape, F32), pltpu.VMEM(z_shape, F32)],
        compiler_params=_params(("arbitrary",)),
        name="ffn_up",
    )(x, x, x, norm_w.reshape(1, D), w_up, w_up, conv_w, conv_w, conv_b, conv_b)


def _ffn_down_kernel(x_ref, a_ref, w_ref, out_ref):
    full = (D_FF // FF_TILE) * FF_TILE
    acc = _dot(a_ref[:, :full], w_ref[:full, :]) + _dot(a_ref[:, full + FF_DUP:], w_ref[full:, :])
    out_ref[...] = x_ref[...] + acc


def _ffn_down(x, act, w_down, l, tm, tn):
    T, D = x.shape
    K = act.shape[1]
    return pl.pallas_call(
        _ffn_down_kernel,
        grid=(T // tm, D // tn),
        in_specs=[
            pl.BlockSpec((tm, tn), lambda i, j: (i, j)),
            pl.BlockSpec((tm, K), lambda i, j: (i, 0)),
            pl.BlockSpec((None, D_FF, tn), lambda i, j: (l, 0, j)),
        ],
        out_specs=pl.BlockSpec((tm, tn), lambda i, j: (i, j)),
        out_shape=jax.ShapeDtypeStruct((T, D), F32),
        compiler_params=_params(("arbitrary", "arbitrary")),
        name="ffn_down",
    )(x, act, w_down)


def _final_norm_kernel(x_ref, w_ref, out_ref):
    xf = x_ref[...]
    ms = jnp.mean(xf * xf, axis=-1, keepdims=True)
    out_ref[...] = xf * lax.rsqrt(ms + EPS) * w_ref[...]


def _final_norm(x, w, tm):
    T, D = x.shape
    return pl.pallas_call(
        _final_norm_kernel,
        grid=(T // tm,),
        in_specs=[pl.BlockSpec((tm, D), lambda i: (i, 0)), pl.BlockSpec((1, D), lambda i: (0, 0))],
        out_specs=pl.BlockSpec((tm, D), lambda i: (i, 0)),
        out_shape=jax.ShapeDtypeStruct((T, D), F32),
        compiler_params=_params(("arbitrary",)),
        name="final_norm",
    )(x, w.reshape(1, D))


def _split_in_proj(w_in):
    w_main = jnp.concatenate([w_in[..., a:b] for a, b in IN_PROJ_SLICES], axis=-1).astype(BF16)
    n_gate = GATE_SLICE[1] - GATE_SLICE[0]
    zeros = jnp.zeros(w_in.shape[:-1] + (GATE_PAD - n_gate,), BF16)
    w_gate = jnp.concatenate([w_in[..., GATE_SLICE[0]:GATE_SLICE[1]].astype(BF16), zeros], axis=-1)
    return w_main, w_gate


def _tile(T, want):
    return min(want, T)


def kernel(x, norm1_w, w_in, mlstm_conv_w, mlstm_conv_b, mlstm_i_bias, mlstm_f_bias, mlstm_norm_w,
           hgrn_lb_logits, hgrn_f_bias, hgrn_norm_w, sgu_ln_w, sgu_ln_b, sgu_w, sgu_b, na_rel_bias,
           out_norm_w, w_out, norm2_w, ffn_w_up, ffn_conv_w, ffn_conv_b, ffn_w_down, final_norm_w):
    B, T, D = x.shape
    assert B == 1 and D == D_MODEL and T % (NA_ROWS * GRID_W) == 0
    depth = w_in.shape[0]
    tm = _tile(T, 1024)
    G = _tile(T, 512)

    lb_all = jax.nn.softmax(hgrn_lb_logits.astype(F32), axis=0)
    lower_bounds = jnp.cumsum(lb_all, axis=0) - lb_all[0]

    w_main, w_gate = _split_in_proj(w_in)
    w_out_b = w_out.astype(BF16)
    w_up = ffn_w_up.astype(BF16)
    conv_w = ffn_conv_w.astype(F32)
    conv_b = ffn_conv_b.astype(F32)[:, None, :]
    w_dn = ffn_w_down.astype(BF16)
    bands = _na_bias_bands(na_rel_bias)

    xs = x[0]
    for l in range(depth):
        proj, proj_b, gates = _proj_in(xs, norm1_w[l], w_main, w_gate, l, tm)
        y_a = _mlstm(proj, proj_b, gates, mlstm_conv_w[l], mlstm_conv_b[l], mlstm_i_bias[l], mlstm_f_bias[l],
                     mlstm_norm_w[l], G)
        y_b = _hgrn(proj, proj_b, lower_bounds[l], hgrn_f_bias[l], hgrn_norm_w[l], G)
        y_c = _sgu(proj, sgu_ln_w[l], sgu_ln_b[l], sgu_w[l], sgu_b[l], out_norm_w[l, :W_GRP], G)
        y_d = _na(proj_b, bands, out_norm_w[l, W_GRP:], l, min(NA_ROW_BLOCK, T // GRID_W))
        xs = _proj_out(xs, (y_a, y_b, y_c, y_d), w_out_b, l, tm, 1024)
        act = _ffn_up(xs, norm2_w[l], w_up, conv_w, conv_b, l, tm)
        xs = _ffn_down(xs, act, w_dn, l, tm, 512)
    return _final_norm(xs, final_norm_w, tm)[None]
```

```python
import functools

import numpy as np
import jax
import jax.numpy as jnp
from jax import lax
from jax.experimental import pallas as pl
from jax.experimental.pallas import tpu as pltpu

F32 = jnp.float32
BF16 = jnp.bfloat16
HIGHEST = lax.Precision.HIGHEST

D_MODEL = 2048
HEAD_DIM = 128
N_HEADS = 4
W_GRP = N_HEADS * HEAD_DIM
D_FF = 5504
GRID_W = 64
NA_ROWS = 8
NA_COLS = 16
NA_ROW_BLOCK = 32
EPS = 1e-6
NEG_BIG = -1e30

LIN_CHUNK = 128
SGU_CHUNK = 128
GATE_PAD = 128
FF_TILE = 512
FF_SUB = 256
FF_DUP = -D_FF % FF_TILE
FF_MB = 512
FF_KC = 256
FF_RB = 16
HALO = 8
LANE = 128

COL_A_Q, COL_A_K = 0, 1
COL_B_Q, COL_B_FF, COL_B_FB, COL_B_G = 2, 3, 4, 5
COL_C_U, COL_C_V = 6, 7
N_F32 = 8
COL_A_V, COL_A_O, COL_B_I, COL_D_Q, COL_D_K, COL_D_V = 0, 1, 2, 3, 4, 5
N_BF16 = 6
GATE_SLICE = (2048, 2064)
PROJ_ORDER = (0, 1, 4, 5, 6, 8, 9, 10, 2, 3, 7, 11, 12, 13)
PROJ_TILE = 2 * W_GRP

VMEM_LIMIT = 56 * 1024 * 1024

_NT = (((1,), (1,)), ((), ()))
_TN = (((0,), (0,)), ((), ()))


def _dot(a, b, dims=None, precision=None):
    if dims is None:
        return jnp.dot(a, b, preferred_element_type=F32, precision=precision)
    return lax.dot_general(a, b, dims, preferred_element_type=F32, precision=precision)


def _bdot(a, b, dims=None):
    return _dot(a.astype(BF16), b.astype(BF16), dims)


def _sigmoid(x):
    return jax.nn.sigmoid(x)


def _log_sigmoid(x):
    return jnp.minimum(x, 0.0) - jnp.log1p(jnp.exp(-jnp.abs(x)))


def _neg_abs(x):
    return -jnp.abs(x)


def _head_mean(x):
    hi = x.astype(BF16)
    lo = (x - hi.astype(F32)).astype(BF16)
    avg = jnp.full((HEAD_DIM, HEAD_DIM), 1.0 / HEAD_DIM, BF16)
    return _dot(hi, avg) + _dot(lo, avg)


def _params(sem):
    return pltpu.CompilerParams(dimension_semantics=sem, vmem_limit_bytes=VMEM_LIMIT)


def _proj_in_kernel(x_ref, nw_ref, wl_ref, wr_ref, wg_ref, pf_ref, pb_ref, gates_ref, h_ref):
    j = pl.program_id(1)

    @pl.when(j == 0)
    def _():
        xf = x_ref[...]
        ms = jnp.mean(xf * xf, axis=-1, keepdims=True)
        h = (xf * lax.rsqrt(ms + EPS) * nw_ref[...]).astype(BF16)
        h_ref[...] = h
        gates_ref[...] = _dot(h, wg_ref[...])

    n_f32_tiles = N_F32 * W_GRP // PROJ_TILE

    @pl.when(j < n_f32_tiles)
    def _():
        pf_ref[:, :W_GRP] = _dot(h_ref[...], wl_ref[...])
        pf_ref[:, W_GRP:] = _dot(h_ref[...], wr_ref[...])

    @pl.when(j >= n_f32_tiles)
    def _():
        pb_ref[:, :W_GRP] = _dot(h_ref[...], wl_ref[...]).astype(BF16)
        pb_ref[:, W_GRP:] = _dot(h_ref[...], wr_ref[...]).astype(BF16)


def _proj_in(x, norm_w, w_main, w_gate, l, tm):
    T, D = x.shape
    nf = N_F32 * W_GRP // PROJ_TILE
    nb = N_BF16 * W_GRP // PROJ_TILE

    def src_block(j, side):
        blk = PROJ_ORDER[side]
        for t in range(1, nf + nb):
            blk = jnp.where(j == t, PROJ_ORDER[2 * t + side], blk)
        return blk

    return pl.pallas_call(
        _proj_in_kernel,
        grid=(T // tm, nf + nb),
        in_specs=[
            pl.BlockSpec((tm, D), lambda i, j: (i, 0)),
            pl.BlockSpec((1, D), lambda i, j: (0, 0)),
            pl.BlockSpec((None, D, W_GRP), lambda i, j: (l, 0, src_block(j, 0))),
            pl.BlockSpec((None, D, W_GRP), lambda i, j: (l, 0, src_block(j, 1))),
            pl.BlockSpec((None, D, GATE_PAD), lambda i, j: (l, 0, 0)),
        ],
        out_specs=[
            pl.BlockSpec((tm, PROJ_TILE), lambda i, j: (i, jnp.minimum(j, nf - 1))),
            pl.BlockSpec((tm, PROJ_TILE), lambda i, j: (i, jnp.maximum(j - nf, 0))),
            pl.BlockSpec((tm, GATE_PAD), lambda i, j: (i, 0)),
        ],
        out_shape=[
            jax.ShapeDtypeStruct((T, N_F32 * W_GRP), F32),
            jax.ShapeDtypeStruct((T, N_BF16 * W_GRP), BF16),
            jax.ShapeDtypeStruct((T, GATE_PAD), F32),
        ],
        scratch_shapes=[pltpu.VMEM((tm, D), BF16)],
        compiler_params=_params(("arbitrary", "arbitrary")),
        name="proj_in",
    )(x, norm_w.reshape(1, D), w_main, w_main, w_gate)


def _causal_mask(L, reverse):
    t = np.arange(L)[:, None]
    s = np.arange(L)[None, :]
    return ((s >= t) if reverse else (s <= t)).astype(np.float32)


def _shift_rows(x, prev_grp, next_grp):
    n, w = x.shape
    ng = n // HALO
    sub = lax.broadcasted_iota(jnp.int32, (1, HALO, 1), 1)
    x3 = jnp.concatenate([prev_grp, x, next_grp], axis=0).reshape(ng + 2, HALO, w)
    down = pltpu.roll(x3, 1, 1)
    up = pltpu.roll(x3, HALO - 1, 1)
    xm1 = jnp.where(sub == 0, down[0:ng], down[1:ng + 1])
    xp1 = jnp.where(sub == HALO - 1, up[2:ng + 2], up[1:ng + 1])
    return xm1.reshape(n, w), xp1.reshape(n, w)


def _dwconv3(x, prev_grp, next_grp, w3, b):
    xm1, xp1 = _shift_rows(x, prev_grp, next_grp)
    return xm1 * w3[0:1] + x * w3[1:2] + xp1 * w3[2:3] + b


def _halo_specs(G, T, width, col, rev):
    nb = T // G
    per = G // HALO
    last = T // HALO - 1

    def blk(c):
        return (nb - 1 - c) if rev else c

    prev = pl.BlockSpec((HALO, width), lambda c: (jnp.maximum(blk(c) * per - 1, 0), col))
    nxt = pl.BlockSpec((HALO, width), lambda c: (jnp.minimum((blk(c) + 1) * per, last), col))
    return prev, nxt


def _mlstm_kernel(*refs, reverse, G, L):
    if reverse:
        (q_ref, qp_ref, qn_ref, k_ref, kp_ref, kn_ref, v_ref, g_ref, gt_ref, gb_ref, gbt_ref,
         cw_ref, cb_ref, mask_ref, hf_ref, o_ref, nw_ref, out_ref,
         qc_ref, kc_ref, ct_ref, m_ref) = refs
    else:
        (q_ref, qp_ref, qn_ref, k_ref, kp_ref, kn_ref, v_ref, g_ref, gt_ref, gb_ref, gbt_ref,
         cw_ref, cb_ref, mask_ref, out_ref,
         qc_ref, kc_ref, ct_ref, m_ref) = refs
    c = pl.program_id(0)
    nblk = pl.num_programs(0)
    blk = (nblk - 1 - c) if reverse else c

    @pl.when(c == 0)
    def _():
        ct_ref[...] = jnp.zeros_like(ct_ref)
        m_ref[...] = jnp.zeros_like(m_ref)

    has_prev = (blk > 0).astype(F32)
    has_next = (blk < nblk - 1).astype(F32)
    cw = cw_ref[...]
    cb = cb_ref[...]
    zq = _dwconv3(q_ref[...], qp_ref[...] * has_prev, qn_ref[...] * has_next, cw[:, :W_GRP], cb[:, :W_GRP])
    qc_ref[...] = zq * _sigmoid(zq)
    zk = _dwconv3(k_ref[...], kp_ref[...] * has_prev, kn_ref[...] * has_next, cw[:, W_GRP:], cb[:, W_GRP:])
    kc_ref[...] = zk * _sigmoid(zk) * (HEAD_DIM ** -0.5)

    maskf = mask_ref[...]
    mask = maskf > 0.5
    ones_col = jnp.ones((L, HEAD_DIM), BF16)
    d = 1 if reverse else 0
    nsub = G // L

    def chunk(j, carry):
        jj = (nsub - 1 - j) if reverse else j
        r0 = pl.multiple_of(jj * L, L)
        rows = pl.ds(r0, L)
        gch = g_ref[rows, :] + gb_ref[...]
        b_cols = _dot(maskf, _log_sigmoid(gch), precision=HIGHEST)
        gt = gt_ref[jj] + gbt_ref[...]
        b_rows = _dot(_log_sigmoid(gt), maskf, _NT, precision=HIGHEST)
        heads = range(N_HEADS)
        hcs = [slice(h * HEAD_DIM, (h + 1) * HEAD_DIM) for h in heads]
        qh = [qc_ref[rows, hc].astype(BF16) for hc in hcs]
        kh = [kc_ref[rows, hc].astype(BF16) for hc in hcs]
        v_aug = [jnp.concatenate([v_ref[rows, hc], ones_col], axis=1) for hc in hcs]
        ct = [ct_ref[h] for h in heads]
        m_prev = [m_ref[h][0:1, 0:1] for h in heads]
        qk = [_dot(qh[h], kh[h], _NT) for h in heads]
        qc = [_dot(qh[h], ct[h].astype(BF16)) for h in heads]
        w_intra, w_inter, m_t, decay, upd = [], [], [], [], []
        for h in heads:
            ci = d * N_HEADS + h
            cf = 2 * N_HEADS + ci
            i_col, b_col = gch[:, ci:ci + 1], b_cols[:, cf:cf + 1]
            i_row, b_row = gt[ci:ci + 1, :], b_rows[cf:cf + 1, :]
            dmat = jnp.where(mask, b_col - b_row + i_row, NEG_BIG)
            inter = b_col + m_prev[h]
            m_t.append(jnp.maximum(inter, jnp.max(dmat, axis=1, keepdims=True)))
            w_intra.append(jnp.where(mask, jnp.exp(dmat - m_t[h]), 0.0))
            w_inter.append(jnp.exp(inter - m_t[h]))
            b_last = b_col[0:1] if reverse else b_col[L - 1:L]
            a_col = i_col + b_last - b_col
            m_new = jnp.maximum(b_last + m_prev[h], jnp.max(a_col, axis=0, keepdims=True))
            decay.append(jnp.exp(b_last + m_prev[h] - m_new))
            wk = jnp.exp(a_col - m_new)
            upd.append(_bdot(kh[h], wk * v_aug[h], _TN))
            m_ref[h] = jnp.broadcast_to(m_new, m_ref.shape[1:])
        sv = [_bdot(qk[h] * w_intra[h], v_aug[h]) for h in heads]
        hh = []
        for h in heads:
            ct_ref[h] = decay[h] * ct[h] + upd[h]
            num = sv[h] + w_inter[h] * qc[h]
            den = num[:, HEAD_DIM:]
            hh.append(num[:, :HEAD_DIM] / jnp.maximum(jnp.abs(den), jnp.exp(-m_t[h])))
        if reverse:
            tot = [hf_ref[rows, hcs[h]] + hh[h] for h in heads]
            mu = [_head_mean(tot[h]) for h in heads]
            tc = [tot[h] - mu[h] for h in heads]
            var = [_head_mean(tc[h] * tc[h]) for h in heads]
            for h in heads:
                y = tc[h] * lax.rsqrt(var[h] + EPS) * nw_ref[:, hcs[h]] * _sigmoid(o_ref[rows, hcs[h]].astype(F32))
                out_ref[rows, hcs[h]] = y.astype(out_ref.dtype)
        else:
            for h in heads:
                out_ref[rows, hcs[h]] = hh[h]
        return carry

    lax.fori_loop(0, nsub, chunk, 0, unroll=True)


def _mlstm_dir(proj, proj_b, gates, gates_t, gb, gbt, conv_w, conv_b, reverse, G, hf=None, norm_w=None):
    T = proj.shape[0]
    L = LIN_CHUNK
    nb = T // G

    def blk(c):
        return (nb - 1 - c) if reverse else c

    def col_spec(col):
        return pl.BlockSpec((G, W_GRP), lambda c: (blk(c), col))

    qp, qn = _halo_specs(G, T, W_GRP, COL_A_Q, reverse)
    kp, kn = _halo_specs(G, T, W_GRP, COL_A_K, reverse)
    const2 = lambda shape: pl.BlockSpec(shape, lambda c: (0, 0))
    in_specs = [
        col_spec(COL_A_Q), qp, qn, col_spec(COL_A_K), kp, kn, col_spec(COL_A_V),
        pl.BlockSpec((G, GATE_PAD), lambda c: (blk(c), 0)),
        pl.BlockSpec((G // L, 4 * N_HEADS, L), lambda c: (blk(c), 0, 0)),
        const2((1, GATE_PAD)), const2((4 * N_HEADS, 1)),
        const2((3, 2 * W_GRP)), const2((1, 2 * W_GRP)), const2((L, L)),
    ]
    args = [proj, proj, proj, proj, proj, proj, proj_b, gates, gates_t, gb, gbt,
            conv_w, conv_b.reshape(1, -1), jnp.asarray(_causal_mask(L, reverse))]
    if reverse:
        in_specs += [pl.BlockSpec((G, W_GRP), lambda c: (blk(c), 0)), col_spec(COL_A_O),
                     const2((1, W_GRP))]
        args += [hf, proj_b, norm_w.reshape(1, -1)]
    return pl.pallas_call(
        functools.partial(_mlstm_kernel, reverse=reverse, G=G, L=L),
        grid=(nb,),
        in_specs=in_specs,
        out_specs=pl.BlockSpec((G, W_GRP), lambda c: (blk(c), 0)),
        out_shape=jax.ShapeDtypeStruct((T, W_GRP), BF16 if reverse else F32),
        scratch_shapes=[
            pltpu.VMEM((G, W_GRP), F32), pltpu.VMEM((G, W_GRP), F32),
            pltpu.VMEM((N_HEADS, HEAD_DIM, 2 * HEAD_DIM), F32),
            pltpu.VMEM((N_HEADS, 8, 128), F32),
        ],
        compiler_params=_params(("arbitrary",)),
        name="mlstm_bwd" if reverse else "mlstm_fwd",
    )(*args)


def _mlstm(proj, proj_b, gates, conv_w, conv_b, i_bias, f_bias, norm_w, G):
    T = proj.shape[0]
    L = LIN_CHUNK
    ng = 4 * N_HEADS
    gbias = jnp.concatenate([i_bias.reshape(-1), f_bias.reshape(-1)]).astype(F32)
    gb = jnp.pad(gbias, (0, GATE_PAD - ng)).reshape(1, GATE_PAD)
    gbt = gbias.reshape(ng, 1)
    gates_t = gates[:, :ng].reshape(T // L, L, ng).transpose(0, 2, 1)
    hf = _mlstm_dir(proj, proj_b, gates, gates_t, gb, gbt, conv_w, conv_b, False, G)
    return _mlstm_dir(proj, proj_b, gates, gates_t, gb, gbt, conv_w, conv_b, True, G, hf, norm_w)


def _hgrn_level_masks(L, reverse):
    n_lev = int(np.log2(L))
    t = np.arange(L)
    lmask = []
    for lev in range(n_lev):
        h = 1 << lev
        blk = t // (2 * h)
        upper = (t // h) % 2 == 1
        is_q = ~upper if reverse else upper
        same = blk[:, None] == blk[None, :]
        lmask.append((same & is_q[:, None] & (~is_q)[None, :]).astype(np.float32))
    lmask.append(np.eye(L, dtype=np.float32))
    return np.stack(lmask)


def _anchor_rows(b, lev, reverse):
    L, W = b.shape
    h = 1 << lev
    a = h if reverse else h - 1
    if 2 * h >= 8:
        b3 = b.reshape(L // (2 * h), 2 * h, W)
        return jnp.broadcast_to(b3[:, a:a + 1, :], b3.shape).reshape(L, W)
    b3 = b.reshape(L // 8, 8, W)
    sub = lax.broadcasted_iota(jnp.int32, (1, 8, 1), 1)
    out = None
    for g in range(8 // (2 * h)):
        piece = jnp.broadcast_to(b3[:, g * 2 * h + a:g * 2 * h + a + 1, :], b3.shape)
        out = piece if out is None else jnp.where(sub >= g * 2 * h, piece, out)
    return out.reshape(L, W)


def _hgrn_kernel(*refs, reverse, G, L):
    if reverse:
        (q_ref, f_ref, i_ref, lb_ref, fb_ref, mask_ref, lmask_ref, of_ref, g_ref, nw_ref,
         out_ref, st_ref) = refs
    else:
        (q_ref, f_ref, i_ref, lb_ref, fb_ref, mask_ref, lmask_ref, out_ref, st_ref) = refs
    c = pl.program_id(0)

    @pl.when(c == 0)
    def _():
        st_ref[...] = jnp.zeros_like(st_ref)

    n_lev = lmask_ref.shape[0] - 1
    nsub = G // L
    heads = range(N_HEADS)
    hcs = [slice(h * HEAD_DIM, (h + 1) * HEAD_DIM) for h in heads]
    t_idx = lax.broadcasted_iota(jnp.int32, (L, 1), 0)

    def chunk(j, carry):
        jj = (nsub - 1 - j) if reverse else j
        r0 = pl.multiple_of(jj * L, L)
        rows = pl.ds(r0, L)
        lb = lb_ref[...]
        z = f_ref[rows, :] + fb_ref[...]
        sig = _sigmoid(z)
        logf = jnp.log2(lb + (1.0 - lb) * sig)
        key = (1.0 - lb) * (1.0 - sig)
        qs = q_ref[rows, :]
        qs = qs * _sigmoid(qs)
        b = _dot(mask_ref[...], logf, precision=HIGHEST)
        b_last = b[0:1] if reverse else b[L - 1:L]
        vh = [i_ref[rows, hc].astype(BF16) for hc in hcs]
        st = [st_ref[h] for h in heads]
        q_in = (qs * jnp.exp2(b)).astype(BF16)
        k_out = (key * jnp.exp2(b_last - b)).astype(BF16)
        inter = [_dot(q_in[:, hcs[h]], st[h].astype(BF16), _NT) for h in heads]
        upd = [_dot(vh[h], k_out[:, hcs[h]], _TN) for h in heads]
        qb, kb = qs.astype(BF16), key.astype(BF16)
        scores = [lmask_ref[n_lev] * _dot(qb[:, hcs[h]], kb[:, hcs[h]], _NT) for h in heads]
        for lev in range(n_lev):
            is_q = ((t_idx >> lev) & 1) == (0 if reverse else 1)
            x = (jnp.where(is_q, qs, key) * jnp.exp2(_neg_abs(b - _anchor_rows(b, lev, reverse)))).astype(BF16)
            for h in heads:
                scores[h] = scores[h] + lmask_ref[lev] * _dot(x[:, hcs[h]], x[:, hcs[h]], _NT)
        decay = jnp.exp2(b_last)
        o = []
        for h in heads:
            st_ref[h] = st[h] * decay[:, hcs[h]] + upd[h]
            o.append(_dot(scores[h].astype(BF16), vh[h]) + inter[h])
        if reverse:
            tot = [of_ref[rows, hcs[h]] + o[h] for h in heads]
            for h in heads:
                gg = g_ref[rows, hcs[h]]
                ms = jnp.mean(tot[h] * tot[h], axis=-1, keepdims=True)
                y = tot[h] * lax.rsqrt(ms + EPS) * nw_ref[:, hcs[h]] * (gg * _sigmoid(gg))
                out_ref[rows, hcs[h]] = y.astype(out_ref.dtype)
        else:
            for h in heads:
                out_ref[rows, hcs[h]] = o[h]
        return carry

    lax.fori_loop(0, nsub, chunk, 0, unroll=True)


def _hgrn_dir(proj, proj_b, lb, fb, reverse, G, of=None, norm_w=None):
    T = proj.shape[0]
    L = LIN_CHUNK
    nb = T // G
    mask = _causal_mask(L, reverse)
    lmask = _hgrn_level_masks(L, reverse)

    def blk(c):
        return (nb - 1 - c) if reverse else c

    def col_spec(col):
        return pl.BlockSpec((G, W_GRP), lambda c: (blk(c), col))

    const2 = lambda shape: pl.BlockSpec(shape, lambda c: (0, 0))
    const3 = lambda shape: pl.BlockSpec(shape, lambda c: (0, 0, 0))
    in_specs = [col_spec(COL_B_Q), col_spec(COL_B_FB if reverse else COL_B_FF), col_spec(COL_B_I),
                const2((1, W_GRP)), const2((1, W_GRP)), const2(mask.shape), const3(lmask.shape)]
    args = [proj, proj, proj_b, lb.reshape(1, -1), fb.reshape(1, -1), jnp.asarray(mask), jnp.asarray(lmask)]
    if reverse:
        in_specs += [pl.BlockSpec((G, W_GRP), lambda c: (blk(c), 0)), col_spec(COL_B_G), const2((1, W_GRP))]
        args += [of, proj, norm_w.reshape(1, -1)]
    return pl.pallas_call(
        functools.partial(_hgrn_kernel, reverse=reverse, G=G, L=L),
        grid=(nb,),
        in_specs=in_specs,
        out_specs=pl.BlockSpec((G, W_GRP), lambda c: (blk(c), 0)),
        out_shape=jax.ShapeDtypeStruct((T, W_GRP), BF16 if reverse else F32),
        scratch_shapes=[pltpu.VMEM((N_HEADS, HEAD_DIM, HEAD_DIM), F32)],
        compiler_params=_params(("arbitrary",)),
        name="hgrn_bwd" if reverse else "hgrn_fwd",
    )(*args)


def _hgrn(proj, proj_b, lb, f_bias, norm_w, G):
    of = _hgrn_dir(proj, proj_b, lb, f_bias[0], False, G)
    return _hgrn_dir(proj, proj_b, lb, f_bias[1], True, G, of, norm_w)


def _gelu(x):
    return 0.5 * x * (1.0 + lax.erf(x * (2.0 ** -0.5)))


def _sgu_kernel(u_ref, v_ref, lnw_ref, lnb_ref, ws_ref, bs_ref, nw_ref, out_ref, *, G):
    v = _gelu(v_ref[...])
    mu = jnp.mean(v, axis=-1, keepdims=True)
    vc = v - mu
    var = jnp.mean(vc * vc, axis=-1, keepdims=True)
    vn = (vc * lax.rsqrt(var + EPS) * lnw_ref[...] + lnb_ref[...]).astype(BF16)
    for n in range(G // SGU_CHUNK):
        rows = slice(n * SGU_CHUNK, (n + 1) * SGU_CHUNK)
        for g in range(N_HEADS):
            hc = slice(g * HEAD_DIM, (g + 1) * HEAD_DIM)
            mixed = _dot(ws_ref[g], vn[rows, hc]) + bs_ref[:, g:g + 1]
            y = _gelu(u_ref[rows, hc]) * mixed
            ms = jnp.mean(y * y, axis=-1, keepdims=True)
            out_ref[rows, hc] = (y * lax.rsqrt(ms + EPS) * nw_ref[:, hc]).astype(out_ref.dtype)


def _sgu(proj, ln_w, ln_b, w_s, b_s, norm_w, G):
    T = proj.shape[0]
    row = lambda a: a.reshape(1, -1)
    const2 = lambda shape: pl.BlockSpec(shape, lambda c: (0, 0))
    return pl.pallas_call(
        functools.partial(_sgu_kernel, G=G),
        grid=(T // G,),
        in_specs=[
            pl.BlockSpec((G, W_GRP), lambda c: (c, COL_C_U)),
            pl.BlockSpec((G, W_GRP), lambda c: (c, COL_C_V)),
            const2((1, W_GRP)), const2((1, W_GRP)),
            pl.BlockSpec((N_HEADS, SGU_CHUNK, SGU_CHUNK), lambda c: (0, 0, 0)),
            const2((SGU_CHUNK, N_HEADS)), const2((1, W_GRP)),
        ],
        out_specs=pl.BlockSpec((G, W_GRP), lambda c: (c, 0)),
        out_shape=jax.ShapeDtypeStruct((T, W_GRP), BF16),
        compiler_params=_params(("arbitrary",)),
        name="sgu",
    )(proj, proj, row(ln_w), row(ln_b), w_s.astype(BF16), b_s.T, row(norm_w))


def _na_bias_bands(rel_bias):
    cols = np.arange(GRID_W)
    col_start = np.clip(cols - NA_COLS // 2, 0, GRID_W - NA_COLS)
    cc = cols[None, :]
    valid = (cc >= col_start[:, None]) & (cc < col_start[:, None] + NA_COLS)
    col_off = cc - cols[:, None] + (NA_COLS - 1)
    n_off = 2 * NA_COLS - 1
    onehot = ((col_off[None] == np.arange(n_off)[:, None, None]) & valid[None]).astype(np.float32)
    lead = rel_bias.shape[:-1]
    bands = jnp.dot(rel_bias.astype(F32).reshape(-1, n_off), jnp.asarray(onehot.reshape(n_off, -1)),
                    precision=HIGHEST)
    bands = bands + jnp.asarray(np.where(valid, 0.0, NEG_BIG).astype(np.float32).reshape(1, -1))
    return bands.reshape(*lead, GRID_W, GRID_W)


def _na_kernel(q_ref, k_ref, v_ref, bands_ref, nw_ref, out_ref, bias_ref, *, RB, n_rows):
    R = pl.program_id(1)
    win = NA_ROWS * GRID_W
    scale = HEAD_DIM ** -0.5

    @pl.when(R == 0)
    def _():
        for e in range(NA_ROWS):
            for j in range(NA_ROWS):
                bias_ref[e, :, j * GRID_W:(j + 1) * GRID_W] = bands_ref[j - e + NA_ROWS - 1]

    krows, es = [], []
    for i in range(RB):
        r = R * RB + i
        rs = jnp.clip(r - NA_ROWS // 2, 0, n_rows - NA_ROWS)
        es.append(r - rs)
        krows.append(pl.ds(pl.multiple_of(rs * GRID_W, GRID_W), win))
    qrows = [slice(i * GRID_W, (i + 1) * GRID_W) for i in range(RB)]
    ss = [_bdot(q_ref[qrows[i], :], k_ref[krows[i], :], _NT) for i in range(RB)]
    ps, sums = [], []
    for i in range(RB):
        s = ss[i] * scale + bias_ref[es[i]]
        p = jnp.exp(s - jnp.max(s, axis=-1, keepdims=True))
        sums.append(jnp.sum(p, axis=-1, keepdims=True))
        ps.append(p.astype(BF16))
    os_ = [_dot(ps[i], v_ref[krows[i], :].astype(BF16)) for i in range(RB)]
    for i in range(RB):
        o = os_[i] / sums[i]
        ms = jnp.mean(o * o, axis=-1, keepdims=True)
        out_ref[qrows[i], :] = (o * lax.rsqrt(ms + EPS) * nw_ref[...]).astype(out_ref.dtype)


def _na(proj_b, bands, norm_w, l, RB):
    T = proj_b.shape[0]
    n_rows = T // GRID_W
    cq, ck, cv = (c * N_HEADS for c in (COL_D_Q, COL_D_K, COL_D_V))
    return pl.pallas_call(
        functools.partial(_na_kernel, RB=RB, n_rows=n_rows),
        grid=(N_HEADS, n_rows // RB),
        in_specs=[
            pl.BlockSpec((RB * GRID_W, HEAD_DIM), lambda h, r: (r, cq + h)),
            pl.BlockSpec((T, HEAD_DIM), lambda h, r: (0, ck + h)),
            pl.BlockSpec((T, HEAD_DIM), lambda h, r: (0, cv + h)),
            pl.BlockSpec((None, None, 2 * NA_ROWS - 1, GRID_W, GRID_W), lambda h, r: (l, h, 0, 0, 0)),
            pl.BlockSpec((1, HEAD_DIM), lambda h, r: (0, h)),
        ],
        out_specs=pl.BlockSpec((RB * GRID_W, HEAD_DIM), lambda h, r: (r, h)),
        out_shape=jax.ShapeDtypeStruct((T, W_GRP), BF16),
        scratch_shapes=[pltpu.VMEM((NA_ROWS, GRID_W, NA_ROWS * GRID_W), F32)],
        compiler_params=_params(("arbitrary", "arbitrary")),
        name="na",
    )(proj_b, proj_b, proj_b, bands, norm_w.reshape(1, -1))


def _proj_out_kernel(x_ref, ya_ref, yb_ref, yc_ref, yd_ref, wa_ref, wb_ref, wc_ref, wd_ref, out_ref):
    acc = x_ref[...]
    for y_ref, w_ref in ((ya_ref, wa_ref), (yb_ref, wb_ref), (yc_ref, wc_ref), (yd_ref, wd_ref)):
        acc = acc + _dot(y_ref[...], w_ref[...])
    out_ref[...] = acc


def _proj_out(x, ys, w_out, l, tm, tn):
    T, D = x.shape
    y_spec = pl.BlockSpec((tm, W_GRP), lambda i, j: (i, 0))
    w_specs = [pl.BlockSpec((None, W_GRP, tn), functools.partial(lambda i, j, g: (l, g, j), g=g)) for g in range(4)]
    return pl.pallas_call(
        _proj_out_kernel,
        grid=(T // tm, D // tn),
        in_specs=[pl.BlockSpec((tm, tn), lambda i, j: (i, j))] + [y_spec] * 4 + w_specs,
        out_specs=pl.BlockSpec((tm, tn), lambda i, j: (i, j)),
        out_shape=jax.ShapeDtypeStruct((T, D), F32),
        compiler_params=_params(("arbitrary", "arbitrary")),
        name="proj_out",
    )(x, *ys, w_out, w_out, w_out, w_out)


def _ffn_up_kernel(x_ref, xp_ref, xn_ref, nw_ref, wa_ref, wg_ref, cwa_ref, cwg_ref, cba_ref, cbg_ref,
                   out_ref, h_ref, z0_ref, z1_ref, *, tm, n_tiles, n_steps):
    n = pl.program_id(0)
    nblk = n_steps // n_tiles
    i = jnp.minimum(n // n_tiles, nblk - 1)
    d_model = h_ref.shape[1]
    n_mb = tm // FF_MB
    n_sub = FF_TILE // FF_SUB
    n_k = d_model // FF_KC
    n_rb = FF_MB // FF_RB
    ng = FF_RB // HALO
    sub = lax.broadcasted_iota(jnp.int32, (1, HALO, 1), 1)

    def norm(xf):
        ms = jnp.mean(xf * xf, axis=-1, keepdims=True)
        return xf * lax.rsqrt(ms + EPS) * nw_ref[...]

    @pl.when(n == 0)
    def _():
        z1_ref[...] = jnp.zeros_like(z1_ref)

    @pl.when((lax.rem(n, n_tiles) == 0) & (n < n_steps))
    def _():
        h_ref[0:HALO, :] = (norm(xp_ref[...]) * (i > 0).astype(F32)).astype(BF16)
        h_ref[HALO:HALO + tm, :] = norm(x_ref[...]).astype(BF16)
        h_ref[HALO + tm:, :] = (norm(xn_ref[...]) * (i < nblk - 1).astype(F32)).astype(BF16)

    def finish_rows(zr_ref, mb, s, r):
        cs = slice(s * FF_SUB, (s + 1) * FF_SUB)

        def conv(half, cw, cb):
            z3 = zr_ref[mb, s, half, r * FF_RB:(r + 1) * FF_RB + 2 * HALO, :].reshape(ng + 2, HALO, FF_SUB)
            down = pltpu.roll(z3, 1, 1)
            up = pltpu.roll(z3, HALO - 1, 1)
            zm1 = jnp.where(sub == 0, down[0:ng], down[1:ng + 1])
            zp1 = jnp.where(sub == HALO - 1, up[2:ng + 2], up[1:ng + 1])
            return (zm1 * cw[0:1] + z3[1:ng + 1] * cw[1:2] + zp1 * cw[2:3] + cb).reshape(FF_RB, FF_SUB)

        a = conv(0, cwa_ref[:, cs], cba_ref[:, cs])
        g = conv(1, cwg_ref[:, cs], cbg_ref[:, cs])
        r0 = mb * FF_MB + r * FF_RB
        out_ref[r0:r0 + FF_RB, cs] = (g * _sigmoid(g) * a).astype(out_ref.dtype)

    def step(zw_ref, zr_ref):
        pieces = [(mb, s, r) for mb in range(n_mb) for s in range(n_sub) for r in range(n_rb)]
        n_mm = n_mb * n_sub * 2 * n_k
        done = m = 0
        for mb in range(n_mb):
            hrows = slice(mb * FF_MB, (mb + 1) * FF_MB + 2 * HALO)
            for s in range(n_sub):
                cs = slice(s * FF_SUB, (s + 1) * FF_SUB)
                for half, w_ref in ((0, wa_ref), (1, wg_ref)):
                    acc = None
                    for kc in range(n_k):
                        ks = slice(kc * FF_KC, (kc + 1) * FF_KC)
                        part = _dot(h_ref[hrows, ks], w_ref[ks, cs])
                        acc = part if acc is None else acc + part
                        m += 1
                        while done * n_mm < m * len(pieces):
                            finish_rows(zr_ref, *pieces[done])
                            done += 1
                    zw_ref[mb, s, half] = acc

    parity = lax.rem(n, 2)
    pl.when(parity == 0)(lambda: step(z0_ref, z1_ref))
    pl.when(parity == 1)(lambda: step(z1_ref, z0_ref))


def _ffn_up(x, norm_w, w_up, conv_w, conv_b, l, tm):
    T, D = x.shape
    n_tiles = pl.cdiv(D_FF, FF_TILE)
    n_steps = (T // tm) * n_tiles
    per = tm // HALO
    last = T // HALO - 1
    El = pl.Element

    def row(n):
        return jnp.minimum(n // n_tiles, T // tm - 1)

    def col(n):
        return jnp.where(n == n_steps, n_tiles - 1, lax.rem(n, n_tiles))

    def prev(n):
        return jnp.maximum(n - 1, 0)

    def off(j, base=0):
        return (base // LANE + jnp.minimum(j * (FF_TILE // LANE), (D_FF - FF_TILE) // LANE)) * LANE

    def halves(rows, which):
        return [pl.BlockSpec((None, El(rows), El(FF_TILE)), lambda n: (l, 0, off(which(n)))),
                pl.BlockSpec((None, El(rows), El(FF_TILE)), lambda n: (l, 0, off(which(n), D_FF)))]

    prev_col = lambda n: lax.rem(prev(n), n_tiles)
    z_shape = (tm // FF_MB, FF_TILE // FF_SUB, 2, FF_MB + 2 * HALO, FF_SUB)
    return pl.pallas_call(
        functools.partial(_ffn_up_kernel, tm=tm, n_tiles=n_tiles, n_steps=n_steps),
        grid=(n_steps + 1,),
        in_specs=[
            pl.BlockSpec((tm, D), lambda n: (row(n), 0)),
            pl.BlockSpec((HALO, D), lambda n: (jnp.maximum(row(n) * per - 1, 0), 0)),
            pl.BlockSpec((HALO, D), lambda n: (jnp.minimum((row(n) + 1) * per, last), 0)),
            pl.BlockSpec((1, D), lambda n: (0, 0)),
        ] + halves(D, col) + halves(3, prev_col) + halves(1, prev_col),
        out_specs=pl.BlockSpec((tm, FF_TILE), lambda n: (prev(n) // n_tiles, prev_col(n))),
        out_shape=jax.ShapeDtypeStruct((T, n_tiles * FF_TILE), BF16),
        scratch_shapes=[pltpu.VMEM((tm + 2 * HALO, D), BF16), pltpu.VMEM(z_shape, F32), pltpu.VMEM(z_shape, F32)],
        compiler_params=_params(("arbitrary",)),
        name="ffn_up",
    )(x, x, x, norm_w.reshape(1, D), w_up, w_up, conv_w, conv_w, conv_b, conv_b)


def _ffn_down_kernel(x_ref, a_ref, w_ref, out_ref):
    full = (D_FF // FF_TILE) * FF_TILE
    acc = _dot(a_ref[:, :full], w_ref[:full, :]) + _dot(a_ref[:, full + FF_DUP:], w_ref[full:, :])
    out_ref[...] = x_ref[...] + acc


def _ffn_down(x, act, w_down, l, tm, tn):
    T, D = x.shape
    K = act.shape[1]
    return pl.pallas_call(
        _ffn_down_kernel,
        grid=(T // tm, D // tn),
        in_specs=[
            pl.BlockSpec((tm, tn), lambda i, j: (i, j)),
            pl.BlockSpec((tm, K), lambda i, j: (i, 0)),
            pl.BlockSpec((None, D_FF, tn), lambda i, j: (l, 0, j)),
        ],
        out_specs=pl.BlockSpec((tm, tn), lambda i, j: (i, j)),
        out_shape=jax.ShapeDtypeStruct((T, D), F32),
        compiler_params=_params(("arbitrary", "arbitrary")),
        name="ffn_down",
    )(x, act, w_down)


def _final_norm_kernel(x_ref, w_ref, out_ref):
    xf = x_ref[...]
    ms = jnp.mean(xf * xf, axis=-1, keepdims=True)
    out_ref[...] = xf * lax.rsqrt(ms + EPS) * w_ref[...]


def _final_norm(x, w, tm):
    T, D = x.shape
    return pl.pallas_call(
        _final_norm_kernel,
        grid=(T // tm,),
        in_specs=[pl.BlockSpec((tm, D), lambda i: (i, 0)), pl.BlockSpec((1, D), lambda i: (0, 0))],
        out_specs=pl.BlockSpec((tm, D), lambda i: (i, 0)),
        out_shape=jax.ShapeDtypeStruct((T, D), F32),
        compiler_params=_params(("arbitrary",)),
        name="final_norm",
    )(x, w.reshape(1, D))


def _split_in_proj(w_in):
    w = w_in.astype(BF16)
    w_main = jnp.concatenate([w[..., :GATE_SLICE[0]], w[..., GATE_SLICE[1]:]], axis=-1)
    n_gate = GATE_SLICE[1] - GATE_SLICE[0]
    zeros = jnp.zeros(w_in.shape[:-1] + (GATE_PAD - n_gate,), BF16)
    w_gate = jnp.concatenate([w_in[..., GATE_SLICE[0]:GATE_SLICE[1]].astype(BF16), zeros], axis=-1)
    return w_main, w_gate


def _tile(T, want):
    return min(want, T)


def kernel(x, norm1_w, w_in, mlstm_conv_w, mlstm_conv_b, mlstm_i_bias, mlstm_f_bias, mlstm_norm_w,
           hgrn_lb_logits, hgrn_f_bias, hgrn_norm_w, sgu_ln_w, sgu_ln_b, sgu_w, sgu_b, na_rel_bias,
           out_norm_w, w_out, norm2_w, ffn_w_up, ffn_conv_w, ffn_conv_b, ffn_w_down, final_norm_w):
    B, T, D = x.shape
    assert B == 1 and D == D_MODEL and T % (NA_ROWS * GRID_W) == 0
    depth = w_in.shape[0]
    tm = _tile(T, 1024)
    G = _tile(T, 512)

    lb_all = jax.nn.softmax(hgrn_lb_logits.astype(F32), axis=0)
    lower_bounds = jnp.cumsum(lb_all, axis=0) - lb_all[0]

    w_main, w_gate = _split_in_proj(w_in)
    w_out_b = w_out.astype(BF16)
    w_up = ffn_w_up.astype(BF16)
    conv_w = ffn_conv_w.astype(F32)
    conv_b = ffn_conv_b.astype(F32)[:, None, :]
    w_dn = ffn_w_down.astype(BF16)
    bands = _na_bias_bands(na_rel_bias)

    xs = x[0]
    for l in range(depth):
        proj, proj_b, gates = _proj_in(xs, norm1_w[l], w_main, w_gate, l, tm)
        y_a = _mlstm(proj, proj_b, gates, mlstm_conv_w[l], mlstm_conv_b[l], mlstm_i_bias[l], mlstm_f_bias[l],
                     mlstm_norm_w[l], G)
        y_b = _hgrn(proj, proj_b, lower_bounds[l], hgrn_f_bias[l], hgrn_norm_w[l], G)
        y_c = _sgu(proj, sgu_ln_w[l], sgu_ln_b[l], sgu_w[l], sgu_b[l], out_norm_w[l, :W_GRP], G)
        y_d = _na(proj_b, bands, out_norm_w[l, W_GRP:], l, min(NA_ROW_BLOCK, T // GRID_W))
        xs = _proj_out(xs, (y_a, y_b, y_c, y_d), w_out_b, l, tm, 1024)
        act = _ffn_up(xs, norm2_w[l], w_up, conv_w, conv_b, l, tm)
        xs = _ffn_down(xs, act, w_dn, l, tm, 512)
    return _final_norm(xs, final_norm_w, tm)[None]
```

```python
import functools

import numpy as np
import jax
import jax.numpy as jnp
from jax import lax
from jax.experimental import pallas as pl
from jax.experimental.pallas import tpu as pltpu

F32 = jnp.float32
BF16 = jnp.bfloat16
HIGHEST = lax.Precision.HIGHEST

D_MODEL = 2048
HEAD_DIM = 128
N_HEADS = 4
W_GRP = N_HEADS * HEAD_DIM
D_FF = 5504
GRID_W = 64
NA_ROWS = 8
NA_COLS = 16
NA_ROW_BLOCK = 32
EPS = 1e-6
NEG_BIG = -1e30

LIN_CHUNK = 128
SGU_CHUNK = 128
GATE_PAD = 128
FF_TILE = 512
FF_SUB = 256
FF_DUP = -D_FF % FF_TILE
FF_MB = 512
FF_KC = 256
FF_RB = 32
HALO = 8
LANE = 128

COL_A_Q, COL_A_K = 0, 1
COL_B_Q, COL_B_FF, COL_B_FB, COL_B_G = 2, 3, 4, 5
COL_C_U, COL_C_V = 6, 7
N_F32 = 8
COL_A_V, COL_A_O, COL_B_I, COL_D_Q, COL_D_K, COL_D_V = 0, 1, 2, 3, 4, 5
N_BF16 = 6
GATE_SLICE = (2048, 2064)
PROJ_ORDER = (0, 1, 4, 5, 6, 8, 9, 10, 2, 3, 7, 11, 12, 13)
PROJ_TILE = 2 * W_GRP

VMEM_LIMIT = 56 * 1024 * 1024

_NT = (((1,), (1,)), ((), ()))
_TN = (((0,), (0,)), ((), ()))


def _dot(a, b, dims=None, precision=None):
    if dims is None:
        return jnp.dot(a, b, preferred_element_type=F32, precision=precision)
    return lax.dot_general(a, b, dims, preferred_element_type=F32, precision=precision)


def _bdot(a, b, dims=None):
    return _dot(a.astype(BF16), b.astype(BF16), dims)


def _sigmoid(x):
    return jax.nn.sigmoid(x)


def _log_sigmoid(x):
    return jnp.minimum(x, 0.0) - jnp.log1p(jnp.exp(-jnp.abs(x)))


def _neg_abs(x):
    return -jnp.abs(x)


def _head_mean(x):
    hi = x.astype(BF16)
    lo = (x - hi.astype(F32)).astype(BF16)
    avg = jnp.full((HEAD_DIM, HEAD_DIM), 1.0 / HEAD_DIM, BF16)
    return _dot(hi, avg) + _dot(lo, avg)


def _params(sem):
    return pltpu.CompilerParams(dimension_semantics=sem, vmem_limit_bytes=VMEM_LIMIT)


def _proj_in_kernel(x_ref, nw_ref, wl_ref, wr_ref, wg_ref, pf_ref, pb_ref, gates_ref, h_ref):
    j = pl.program_id(1)

    @pl.when(j == 0)
    def _():
        xf = x_ref[...]
        ms = jnp.mean(xf * xf, axis=-1, keepdims=True)
        h = (xf * lax.rsqrt(ms + EPS) * nw_ref[...]).astype(BF16)
        h_ref[...] = h
        gates_ref[...] = _dot(h, wg_ref[...])

    n_f32_tiles = N_F32 * W_GRP // PROJ_TILE

    @pl.when(j < n_f32_tiles)
    def _():
        pf_ref[:, :W_GRP] = _dot(h_ref[...], wl_ref[...])
        pf_ref[:, W_GRP:] = _dot(h_ref[...], wr_ref[...])

    @pl.when(j >= n_f32_tiles)
    def _():
        pb_ref[:, :W_GRP] = _dot(h_ref[...], wl_ref[...]).astype(BF16)
        pb_ref[:, W_GRP:] = _dot(h_ref[...], wr_ref[...]).astype(BF16)


def _proj_in(x, norm_w, w_main, w_gate, l, tm):
    T, D = x.shape
    nf = N_F32 * W_GRP // PROJ_TILE
    nb = N_BF16 * W_GRP // PROJ_TILE

    def src_block(j, side):
        blk = PROJ_ORDER[side]
        for t in range(1, nf + nb):
            blk = jnp.where(j == t, PROJ_ORDER[2 * t + side], blk)
        return blk

    return pl.pallas_call(
        _proj_in_kernel,
        grid=(T // tm, nf + nb),
        in_specs=[
            pl.BlockSpec((tm, D), lambda i, j: (i, 0)),
            pl.BlockSpec((1, D), lambda i, j: (0, 0)),
            pl.BlockSpec((None, D, W_GRP), lambda i, j: (l, 0, src_block(j, 0))),
            pl.BlockSpec((None, D, W_GRP), lambda i, j: (l, 0, src_block(j, 1))),
            pl.BlockSpec((None, D, GATE_PAD), lambda i, j: (l, 0, 0)),
        ],
        out_specs=[
            pl.BlockSpec((tm, PROJ_TILE), lambda i, j: (i, jnp.minimum(j, nf - 1))),
            pl.BlockSpec((tm, PROJ_TILE), lambda i, j: (i, jnp.maximum(j - nf, 0))),
            pl.BlockSpec((tm, GATE_PAD), lambda i, j: (i, 0)),
        ],
        out_shape=[
            jax.ShapeDtypeStruct((T, N_F32 * W_GRP), F32),
            jax.ShapeDtypeStruct((T, N_BF16 * W_GRP), BF16),
            jax.ShapeDtypeStruct((T, GATE_PAD), F32),
        ],
        scratch_shapes=[pltpu.VMEM((tm, D), BF16)],
        compiler_params=_params(("arbitrary", "arbitrary")),
        name="proj_in",
    )(x, norm_w.reshape(1, D), w_main, w_main, w_gate)


def _causal_mask(L, reverse):
    t = np.arange(L)[:, None]
    s = np.arange(L)[None, :]
    return ((s >= t) if reverse else (s <= t)).astype(np.float32)


def _shift_rows(x, prev_grp, next_grp):
    n, w = x.shape
    ng = n // HALO
    sub = lax.broadcasted_iota(jnp.int32, (1, HALO, 1), 1)
    x3 = jnp.concatenate([prev_grp, x, next_grp], axis=0).reshape(ng + 2, HALO, w)
    down = pltpu.roll(x3, 1, 1)
    up = pltpu.roll(x3, HALO - 1, 1)
    xm1 = jnp.where(sub == 0, down[0:ng], down[1:ng + 1])
    xp1 = jnp.where(sub == HALO - 1, up[2:ng + 2], up[1:ng + 1])
    return xm1.reshape(n, w), xp1.reshape(n, w)


def _dwconv3(x, prev_grp, next_grp, w3, b):
    xm1, xp1 = _shift_rows(x, prev_grp, next_grp)
    return xm1 * w3[0:1] + x * w3[1:2] + xp1 * w3[2:3] + b


def _halo_specs(G, T, width, col, rev):
    nb = T // G
    per = G // HALO
    last = T // HALO - 1

    def blk(c):
        return (nb - 1 - c) if rev else c

    prev = pl.BlockSpec((HALO, width), lambda c: (jnp.maximum(blk(c) * per - 1, 0), col))
    nxt = pl.BlockSpec((HALO, width), lambda c: (jnp.minimum((blk(c) + 1) * per, last), col))
    return prev, nxt


def _mlstm_kernel(*refs, reverse, G, L):
    if reverse:
        (q_ref, qp_ref, qn_ref, k_ref, kp_ref, kn_ref, v_ref, g_ref, gt_ref, gb_ref, gbt_ref,
         cw_ref, cb_ref, mask_ref, hf_ref, o_ref, nw_ref, out_ref,
         qc_ref, kc_ref, ct_ref, m_ref) = refs
    else:
        (q_ref, qp_ref, qn_ref, k_ref, kp_ref, kn_ref, v_ref, g_ref, gt_ref, gb_ref, gbt_ref,
         cw_ref, cb_ref, mask_ref, out_ref,
         qc_ref, kc_ref, ct_ref, m_ref) = refs
    c = pl.program_id(0)
    nblk = pl.num_programs(0)
    blk = (nblk - 1 - c) if reverse else c

    @pl.when(c == 0)
    def _():
        ct_ref[...] = jnp.zeros_like(ct_ref)
        m_ref[...] = jnp.zeros_like(m_ref)

    has_prev = (blk > 0).astype(F32)
    has_next = (blk < nblk - 1).astype(F32)
    cw = cw_ref[...]
    cb = cb_ref[...]
    zq = _dwconv3(q_ref[...], qp_ref[...] * has_prev, qn_ref[...] * has_next, cw[:, :W_GRP], cb[:, :W_GRP])
    qc_ref[...] = zq * _sigmoid(zq)
    zk = _dwconv3(k_ref[...], kp_ref[...] * has_prev, kn_ref[...] * has_next, cw[:, W_GRP:], cb[:, W_GRP:])
    kc_ref[...] = zk * _sigmoid(zk) * (HEAD_DIM ** -0.5)

    maskf = mask_ref[...]
    mask = maskf > 0.5
    ones_col = jnp.ones((L, HEAD_DIM), BF16)
    d = 1 if reverse else 0
    nsub = G // L

    def chunk(j, carry):
        jj = (nsub - 1 - j) if reverse else j
        r0 = pl.multiple_of(jj * L, L)
        rows = pl.ds(r0, L)
        gch = g_ref[rows, :] + gb_ref[...]
        b_cols = _dot(maskf, _log_sigmoid(gch), precision=HIGHEST)
        gt = gt_ref[jj] + gbt_ref[...]
        b_rows = _dot(_log_sigmoid(gt), maskf, _NT, precision=HIGHEST)
        heads = range(N_HEADS)
        hcs = [slice(h * HEAD_DIM, (h + 1) * HEAD_DIM) for h in heads]
        qh = [qc_ref[rows, hc].astype(BF16) for hc in hcs]
        kh = [kc_ref[rows, hc].astype(BF16) for hc in hcs]
        v_aug = [jnp.concatenate([v_ref[rows, hc], ones_col], axis=1) for hc in hcs]
        ct = [ct_ref[h] for h in heads]
        m_prev = [m_ref[h][0:1, 0:1] for h in heads]
        qk = [_dot(qh[h], kh[h], _NT) for h in heads]
        qc = [_dot(qh[h], ct[h].astype(BF16)) for h in heads]
        w_intra, w_inter, m_t, decay, upd = [], [], [], [], []
        for h in heads:
            ci = d * N_HEADS + h
            cf = 2 * N_HEADS + ci
            i_col, b_col = gch[:, ci:ci + 1], b_cols[:, cf:cf + 1]
            i_row, b_row = gt[ci:ci + 1, :], b_rows[cf:cf + 1, :]
            dmat = jnp.where(mask, b_col - b_row + i_row, NEG_BIG)
            inter = b_col + m_prev[h]
            m_t.append(jnp.maximum(inter, jnp.max(dmat, axis=1, keepdims=True)))
            w_intra.append(jnp.where(mask, jnp.exp(dmat - m_t[h]), 0.0))
            w_inter.append(jnp.exp(inter - m_t[h]))
            b_last = b_col[0:1] if reverse else b_col[L - 1:L]
            a_col = i_col + b_last - b_col
            m_new = jnp.maximum(b_last + m_prev[h], jnp.max(a_col, axis=0, keepdims=True))
            decay.append(jnp.exp(b_last + m_prev[h] - m_new))
            wk = jnp.exp(a_col - m_new)
            upd.append(_bdot(kh[h], wk * v_aug[h], _TN))
            m_ref[h] = jnp.broadcast_to(m_new, m_ref.shape[1:])
        sv = [_bdot(qk[h] * w_intra[h], v_aug[h]) for h in heads]
        hh = []
        for h in heads:
            ct_ref[h] = decay[h] * ct[h] + upd[h]
            num = sv[h] + w_inter[h] * qc[h]
            den = num[:, HEAD_DIM:]
            hh.append(num[:, :HEAD_DIM] / jnp.maximum(jnp.abs(den), jnp.exp(-m_t[h])))
        if reverse:
            tot = [hf_ref[rows, hcs[h]] + hh[h] for h in heads]
            mu = [_head_mean(tot[h]) for h in heads]
            tc = [tot[h] - mu[h] for h in heads]
            var = [_head_mean(tc[h] * tc[h]) for h in heads]
            for h in heads:
                y = tc[h] * lax.rsqrt(var[h] + EPS) * nw_ref[:, hcs[h]] * _sigmoid(o_ref[rows, hcs[h]].astype(F32))
                out_ref[rows, hcs[h]] = y.astype(out_ref.dtype)
        else:
            for h in heads:
                out_ref[rows, hcs[h]] = hh[h]
        return carry

    lax.fori_loop(0, nsub, chunk, 0, unroll=True)


def _mlstm_dir(proj, proj_b, gates, gates_t, gb, gbt, conv_w, conv_b, reverse, G, hf=None, norm_w=None):
    T = proj.shape[0]
    L = LIN_CHUNK
    nb = T // G

    def blk(c):
        return (nb - 1 - c) if reverse else c

    def col_spec(col):
        return pl.BlockSpec((G, W_GRP), lambda c: (blk(c), col))

    qp, qn = _halo_specs(G, T, W_GRP, COL_A_Q, reverse)
    kp, kn = _halo_specs(G, T, W_GRP, COL_A_K, reverse)
    const2 = lambda shape: pl.BlockSpec(shape, lambda c: (0, 0))
    in_specs = [
        col_spec(COL_A_Q), qp, qn, col_spec(COL_A_K), kp, kn, col_spec(COL_A_V),
        pl.BlockSpec((G, GATE_PAD), lambda c: (blk(c), 0)),
        pl.BlockSpec((G // L, 4 * N_HEADS, L), lambda c: (blk(c), 0, 0)),
        const2((1, GATE_PAD)), const2((4 * N_HEADS, 1)),
        const2((3, 2 * W_GRP)), const2((1, 2 * W_GRP)), const2((L, L)),
    ]
    args = [proj, proj, proj, proj, proj, proj, proj_b, gates, gates_t, gb, gbt,
            conv_w, conv_b.reshape(1, -1), jnp.asarray(_causal_mask(L, reverse))]
    if reverse:
        in_specs += [pl.BlockSpec((G, W_GRP), lambda c: (blk(c), 0)), col_spec(COL_A_O),
                     const2((1, W_GRP))]
        args += [hf, proj_b, norm_w.reshape(1, -1)]
    return pl.pallas_call(
        functools.partial(_mlstm_kernel, reverse=reverse, G=G, L=L),
        grid=(nb,),
        in_specs=in_specs,
        out_specs=pl.BlockSpec((G, W_GRP), lambda c: (blk(c), 0)),
        out_shape=jax.ShapeDtypeStruct((T, W_GRP), BF16 if reverse else F32),
        scratch_shapes=[
            pltpu.VMEM((G, W_GRP), F32), pltpu.VMEM((G, W_GRP), F32),
            pltpu.VMEM((N_HEADS, HEAD_DIM, 2 * HEAD_DIM), F32),
            pltpu.VMEM((N_HEADS, 8, 128), F32),
        ],
        compiler_params=_params(("arbitrary",)),
        name="mlstm_bwd" if reverse else "mlstm_fwd",
    )(*args)


def _mlstm(proj, proj_b, gates, conv_w, conv_b, i_bias, f_bias, norm_w, G):
    T = proj.shape[0]
    L = LIN_CHUNK
    ng = 4 * N_HEADS
    gbias = jnp.concatenate([i_bias.reshape(-1), f_bias.reshape(-1)]).astype(F32)
    gb = jnp.pad(gbias, (0, GATE_PAD - ng)).reshape(1, GATE_PAD)
    gbt = gbias.reshape(ng, 1)
    gates_t = gates[:, :ng].reshape(T // L, L, ng).transpose(0, 2, 1)
    hf = _mlstm_dir(proj, proj_b, gates, gates_t, gb, gbt, conv_w, conv_b, False, G)
    return _mlstm_dir(proj, proj_b, gates, gates_t, gb, gbt, conv_w, conv_b, True, G, hf, norm_w)


def _hgrn_level_masks(L, reverse):
    n_lev = int(np.log2(L))
    t = np.arange(L)
    lmask = []
    for lev in range(n_lev):
        h = 1 << lev
        blk = t // (2 * h)
        upper = (t // h) % 2 == 1
        is_q = ~upper if reverse else upper
        same = blk[:, None] == blk[None, :]
        lmask.append((same & is_q[:, None] & (~is_q)[None, :]).astype(np.float32))
    lmask.append(np.eye(L, dtype=np.float32))
    return np.stack(lmask)


def _anchor_rows(b, lev, reverse):
    L, W = b.shape
    h = 1 << lev
    a = h if reverse else h - 1
    if 2 * h >= 8:
        b3 = b.reshape(L // (2 * h), 2 * h, W)
        return jnp.broadcast_to(b3[:, a:a + 1, :], b3.shape).reshape(L, W)
    b3 = b.reshape(L // 8, 8, W)
    sub = lax.broadcasted_iota(jnp.int32, (1, 8, 1), 1)
    out = None
    for g in range(8 // (2 * h)):
        piece = jnp.broadcast_to(b3[:, g * 2 * h + a:g * 2 * h + a + 1, :], b3.shape)
        out = piece if out is None else jnp.where(sub >= g * 2 * h, piece, out)
    return out.reshape(L, W)


def _hgrn_kernel(*refs, reverse, G, L):
    if reverse:
        (q_ref, f_ref, i_ref, lb_ref, fb_ref, mask_ref, lmask_ref, of_ref, g_ref, nw_ref,
         out_ref, st_ref) = refs
    else:
        (q_ref, f_ref, i_ref, lb_ref, fb_ref, mask_ref, lmask_ref, out_ref, st_ref) = refs
    c = pl.program_id(0)

    @pl.when(c == 0)
    def _():
        st_ref[...] = jnp.zeros_like(st_ref)

    n_lev = lmask_ref.shape[0] - 1
    nsub = G // L
    heads = range(N_HEADS)
    hcs = [slice(h * HEAD_DIM, (h + 1) * HEAD_DIM) for h in heads]
    t_idx = lax.broadcasted_iota(jnp.int32, (L, 1), 0)

    def chunk(j, carry):
        jj = (nsub - 1 - j) if reverse else j
        r0 = pl.multiple_of(jj * L, L)
        rows = pl.ds(r0, L)
        lb = lb_ref[...]
        z = f_ref[rows, :] + fb_ref[...]
        sig = _sigmoid(z)
        logf = jnp.log2(lb + (1.0 - lb) * sig)
        key = (1.0 - lb) * (1.0 - sig)
        qs = q_ref[rows, :]
        qs = qs * _sigmoid(qs)
        b = _dot(mask_ref[...], logf, precision=HIGHEST)
        b_last = b[0:1] if reverse else b[L - 1:L]
        vh = [i_ref[rows, hc].astype(BF16) for hc in hcs]
        st = [st_ref[h] for h in heads]
        q_in = (qs * jnp.exp2(b)).astype(BF16)
        k_out = (key * jnp.exp2(b_last - b)).astype(BF16)
        inter = [_dot(q_in[:, hcs[h]], st[h].astype(BF16), _NT) for h in heads]
        upd = [_dot(vh[h], k_out[:, hcs[h]], _TN) for h in heads]
        qb, kb = qs.astype(BF16), key.astype(BF16)
        scores = [lmask_ref[n_lev] * _dot(qb[:, hcs[h]], kb[:, hcs[h]], _NT) for h in heads]
        for lev in range(n_lev):
            is_q = ((t_idx >> lev) & 1) == (0 if reverse else 1)
            x = (jnp.where(is_q, qs, key) * jnp.exp2(_neg_abs(b - _anchor_rows(b, lev, reverse)))).astype(BF16)
            for h in heads:
                scores[h] = scores[h] + lmask_ref[lev] * _dot(x[:, hcs[h]], x[:, hcs[h]], _NT)
        decay = jnp.exp2(b_last)
        o = []
        for h in heads:
            st_ref[h] = st[h] * decay[:, hcs[h]] + upd[h]
            o.append(_dot(scores[h].astype(BF16), vh[h]) + inter[h])
        if reverse:
            tot = [of_ref[rows, hcs[h]] + o[h] for h in heads]
            for h in heads:
                gg = g_ref[rows, hcs[h]]
                ms = jnp.mean(tot[h] * tot[h], axis=-1, keepdims=True)
                y = tot[h] * lax.rsqrt(ms + EPS) * nw_ref[:, hcs[h]] * (gg * _sigmoid(gg))
                out_ref[rows, hcs[h]] = y.astype(out_ref.dtype)
        else:
            for h in heads:
                out_ref[rows, hcs[h]] = o[h]
        return carry

    lax.fori_loop(0, nsub, chunk, 0, unroll=True)


def _hgrn_dir(proj, proj_b, lb, fb, reverse, G, of=None, norm_w=None):
    T = proj.shape[0]
    L = LIN_CHUNK
    nb = T // G
    mask = _causal_mask(L, reverse)
    lmask = _hgrn_level_masks(L, reverse)

    def blk(c):
        return (nb - 1 - c) if reverse else c

    def col_spec(col):
        return pl.BlockSpec((G, W_GRP), lambda c: (blk(c), col))

    const2 = lambda shape: pl.BlockSpec(shape, lambda c: (0, 0))
    const3 = lambda shape: pl.BlockSpec(shape, lambda c: (0, 0, 0))
    in_specs = [col_spec(COL_B_Q), col_spec(COL_B_FB if reverse else COL_B_FF), col_spec(COL_B_I),
                const2((1, W_GRP)), const2((1, W_GRP)), const2(mask.shape), const3(lmask.shape)]
    args = [proj, proj, proj_b, lb.reshape(1, -1), fb.reshape(1, -1), jnp.asarray(mask), jnp.asarray(lmask)]
    if reverse:
        in_specs += [pl.BlockSpec((G, W_GRP), lambda c: (blk(c), 0)), col_spec(COL_B_G), const2((1, W_GRP))]
        args += [of, proj, norm_w.reshape(1, -1)]
    return pl.pallas_call(
        functools.partial(_hgrn_kernel, reverse=reverse, G=G, L=L),
        grid=(nb,),
        in_specs=in_specs,
        out_specs=pl.BlockSpec((G, W_GRP), lambda c: (blk(c), 0)),
        out_shape=jax.ShapeDtypeStruct((T, W_GRP), BF16 if reverse else F32),
        scratch_shapes=[pltpu.VMEM((N_HEADS, HEAD_DIM, HEAD_DIM), F32)],
        compiler_params=_params(("arbitrary",)),
        name="hgrn_bwd" if reverse else "hgrn_fwd",
    )(*args)


def _hgrn(proj, proj_b, lb, f_bias, norm_w, G):
    of = _hgrn_dir(proj, proj_b, lb, f_bias[0], False, G)
    return _hgrn_dir(proj, proj_b, lb, f_bias[1], True, G, of, norm_w)


def _gelu(x):
    return 0.5 * x * (1.0 + lax.erf(x * (2.0 ** -0.5)))


def _sgu_kernel(u_ref, v_ref, lnw_ref, lnb_ref, ws_ref, bs_ref, nw_ref, out_ref, *, G):
    v = _gelu(v_ref[...])
    mu = jnp.mean(v, axis=-1, keepdims=True)
    vc = v - mu
    var = jnp.mean(vc * vc, axis=-1, keepdims=True)
    vn = (vc * lax.rsqrt(var + EPS) * lnw_ref[...] + lnb_ref[...]).astype(BF16)
    for n in range(G // SGU_CHUNK):
        rows = slice(n * SGU_CHUNK, (n + 1) * SGU_CHUNK)
        for g in range(N_HEADS):
            hc = slice(g * HEAD_DIM, (g + 1) * HEAD_DIM)
            mixed = _dot(ws_ref[g], vn[rows, hc]) + bs_ref[:, g:g + 1]
            y = _gelu(u_ref[rows, hc]) * mixed
            ms = jnp.mean(y * y, axis=-1, keepdims=True)
            out_ref[rows, hc] = (y * lax.rsqrt(ms + EPS) * nw_ref[:, hc]).astype(out_ref.dtype)


def _sgu(proj, ln_w, ln_b, w_s, b_s, norm_w, G):
    T = proj.shape[0]
    row = lambda a: a.reshape(1, -1)
    const2 = lambda shape: pl.BlockSpec(shape, lambda c: (0, 0))
    return pl.pallas_call(
        functools.partial(_sgu_kernel, G=G),
        grid=(T // G,),
        in_specs=[
            pl.BlockSpec((G, W_GRP), lambda c: (c, COL_C_U)),
            pl.BlockSpec((G, W_GRP), lambda c: (c, COL_C_V)),
            const2((1, W_GRP)), const2((1, W_GRP)),
            pl.BlockSpec((N_HEADS, SGU_CHUNK, SGU_CHUNK), lambda c: (0, 0, 0)),
            const2((SGU_CHUNK, N_HEADS)), const2((1, W_GRP)),
        ],
        out_specs=pl.BlockSpec((G, W_GRP), lambda c: (c, 0)),
        out_shape=jax.ShapeDtypeStruct((T, W_GRP), BF16),
        compiler_params=_params(("arbitrary",)),
        name="sgu",
    )(proj, proj, row(ln_w), row(ln_b), w_s.astype(BF16), b_s.T, row(norm_w))


def _na_bias_bands(rel_bias):
    cols = np.arange(GRID_W)
    col_start = np.clip(cols - NA_COLS // 2, 0, GRID_W - NA_COLS)
    cc = cols[None, :]
    valid = (cc >= col_start[:, None]) & (cc < col_start[:, None] + NA_COLS)
    col_off = cc - cols[:, None] + (NA_COLS - 1)
    n_off = 2 * NA_COLS - 1
    onehot = ((col_off[None] == np.arange(n_off)[:, None, None]) & valid[None]).astype(np.float32)
    lead = rel_bias.shape[:-1]
    bands = jnp.dot(rel_bias.astype(F32).reshape(-1, n_off), jnp.asarray(onehot.reshape(n_off, -1)),
                    precision=HIGHEST)
    bands = bands + jnp.asarray(np.where(valid, 0.0, NEG_BIG).astype(np.float32).reshape(1, -1))
    return bands.reshape(*lead, GRID_W, GRID_W)


def _na_kernel(q_ref, k_ref, v_ref, bands_ref, nw_ref, out_ref, bias_ref, *, RB, n_rows):
    R = pl.program_id(1)
    win = NA_ROWS * GRID_W
    scale = HEAD_DIM ** -0.5

    @pl.when(R == 0)
    def _():
        for e in range(NA_ROWS):
            for j in range(NA_ROWS):
                bias_ref[e, :, j * GRID_W:(j + 1) * GRID_W] = bands_ref[j - e + NA_ROWS - 1]

    krows, es = [], []
    for i in range(RB):
        r = R * RB + i
        rs = jnp.clip(r - NA_ROWS // 2, 0, n_rows - NA_ROWS)
        es.append(r - rs)
        krows.append(pl.ds(pl.multiple_of(rs * GRID_W, GRID_W), win))
    qrows = [slice(i * GRID_W, (i + 1) * GRID_W) for i in range(RB)]
    ss = [_bdot(q_ref[qrows[i], :], k_ref[krows[i], :], _NT) for i in range(RB)]
    ps, sums = [], []
    for i in range(RB):
        s = ss[i] * scale + bias_ref[es[i]]
        p = jnp.exp(s - jnp.max(s, axis=-1, keepdims=True))
        sums.append(jnp.sum(p, axis=-1, keepdims=True))
        ps.append(p.astype(BF16))
    os_ = [_dot(ps[i], v_ref[krows[i], :].astype(BF16)) for i in range(RB)]
    for i in range(RB):
        o = os_[i] / sums[i]
        ms = jnp.mean(o * o, axis=-1, keepdims=True)
        out_ref[qrows[i], :] = (o * lax.rsqrt(ms + EPS) * nw_ref[...]).astype(out_ref.dtype)


def _na(proj_b, bands, norm_w, l, RB):
    T = proj_b.shape[0]
    n_rows = T // GRID_W
    cq, ck, cv = (c * N_HEADS for c in (COL_D_Q, COL_D_K, COL_D_V))
    return pl.pallas_call(
        functools.partial(_na_kernel, RB=RB, n_rows=n_rows),
        grid=(N_HEADS, n_rows // RB),
        in_specs=[
            pl.BlockSpec((RB * GRID_W, HEAD_DIM), lambda h, r: (r, cq + h)),
            pl.BlockSpec((T, HEAD_DIM), lambda h, r: (0, ck + h)),
            pl.BlockSpec((T, HEAD_DIM), lambda h, r: (0, cv + h)),
            pl.BlockSpec((None, None, 2 * NA_ROWS - 1, GRID_W, GRID_W), lambda h, r: (l, h, 0, 0, 0)),
            pl.BlockSpec((1, HEAD_DIM), lambda h, r: (0, h)),
        ],
        out_specs=pl.BlockSpec((RB * GRID_W, HEAD_DIM), lambda h, r: (r, h)),
        out_shape=jax.ShapeDtypeStruct((T, W_GRP), BF16),
        scratch_shapes=[pltpu.VMEM((NA_ROWS, GRID_W, NA_ROWS * GRID_W), F32)],
        compiler_params=_params(("arbitrary", "arbitrary")),
        name="na",
    )(proj_b, proj_b, proj_b, bands, norm_w.reshape(1, -1))


def _proj_out_kernel(x_ref, ya_ref, yb_ref, yc_ref, yd_ref, wa_ref, wb_ref, wc_ref, wd_ref, out_ref):
    acc = x_ref[...]
    for y_ref, w_ref in ((ya_ref, wa_ref), (yb_ref, wb_ref), (yc_ref, wc_ref), (yd_ref, wd_ref)):
        acc = acc + _dot(y_ref[...], w_ref[...])
    out_ref[...] = acc


def _proj_out(x, ys, w_out, l, tm, tn):
    T, D = x.shape
    y_spec = pl.BlockSpec((tm, W_GRP), lambda i, j: (i, 0))
    w_specs = [pl.BlockSpec((None, W_GRP, tn), functools.partial(lambda i, j, g: (l, g, j), g=g)) for g in range(4)]
    return pl.pallas_call(
        _proj_out_kernel,
        grid=(T // tm, D // tn),
        in_specs=[pl.BlockSpec((tm, tn), lambda i, j: (i, j))] + [y_spec] * 4 + w_specs,
        out_specs=pl.BlockSpec((tm, tn), lambda i, j: (i, j)),
        out_shape=jax.ShapeDtypeStruct((T, D), F32),
        compiler_params=_params(("arbitrary", "arbitrary")),
        name="proj_out",
    )(x, *ys, w_out, w_out, w_out, w_out)


def _ffn_up_kernel(x_ref, xp_ref, xn_ref, nw_ref, wa_ref, wg_ref, cwa_ref, cwg_ref, cba_ref, cbg_ref,
                   out_ref, h_ref, z0_ref, z1_ref, *, tm, n_tiles, n_steps):
    n = pl.program_id(0)
    nblk = n_steps // n_tiles
    i = jnp.minimum(n // n_tiles, nblk - 1)
    d_model = h_ref.shape[1]
    n_mb = tm // FF_MB
    n_sub = FF_TILE // FF_SUB
    n_k = d_model // FF_KC
    n_rb = FF_MB // FF_RB
    ng = FF_RB // HALO
    sub = lax.broadcasted_iota(jnp.int32, (1, HALO, 1), 1)

    def norm(xf):
        ms = jnp.mean(xf * xf, axis=-1, keepdims=True)
        return xf * lax.rsqrt(ms + EPS) * nw_ref[...]

    @pl.when(n == 0)
    def _():
        z1_ref[...] = jnp.zeros_like(z1_ref)

    @pl.when((lax.rem(n, n_tiles) == 0) & (n < n_steps))
    def _():
        h_ref[0:HALO, :] = (norm(xp_ref[...]) * (i > 0).astype(F32)).astype(BF16)
        h_ref[HALO:HALO + tm, :] = norm(x_ref[...]).astype(BF16)
        h_ref[HALO + tm:, :] = (norm(xn_ref[...]) * (i < nblk - 1).astype(F32)).astype(BF16)

    def finish_rows(zr_ref, mb, s, r):
        cs = slice(s * FF_SUB, (s + 1) * FF_SUB)

        def conv(half, cw, cb):
            z3 = zr_ref[mb, s, half, r * FF_RB:(r + 1) * FF_RB + 2 * HALO, :].reshape(ng + 2, HALO, FF_SUB)
            down = pltpu.roll(z3, 1, 1)
            up = pltpu.roll(z3, HALO - 1, 1)
            zm1 = jnp.where(sub == 0, down[0:ng], down[1:ng + 1])
            zp1 = jnp.where(sub == HALO - 1, up[2:ng + 2], up[1:ng + 1])
            return (zm1 * cw[0:1] + z3[1:ng + 1] * cw[1:2] + zp1 * cw[2:3] + cb).reshape(FF_RB, FF_SUB)

        a = conv(0, cwa_ref[:, cs], cba_ref[:, cs])
        g = conv(1, cwg_ref[:, cs], cbg_ref[:, cs])
        r0 = mb * FF_MB + r * FF_RB
        out_ref[r0:r0 + FF_RB, cs] = (g * _sigmoid(g) * a).astype(out_ref.dtype)

    def step(zw_ref, zr_ref):
        pieces = [(mb, s, r) for mb in range(n_mb) for s in range(n_sub) for r in range(n_rb)]
        n_mm = n_mb * n_sub * 2 * n_k
        done = m = 0
        for mb in range(n_mb):
            hrows = slice(mb * FF_MB, (mb + 1) * FF_MB + 2 * HALO)
            for s in range(n_sub):
                cs = slice(s * FF_SUB, (s + 1) * FF_SUB)
                for half, w_ref in ((0, wa_ref), (1, wg_ref)):
                    acc = None
                    for kc in range(n_k):
                        ks = slice(kc * FF_KC, (kc + 1) * FF_KC)
                        part = _dot(h_ref[hrows, ks], w_ref[ks, cs])
                        acc = part if acc is None else acc + part
                        m += 1
                        while done * n_mm < m * len(pieces):
                            finish_rows(zr_ref, *pieces[done])
                            done += 1
                    zw_ref[mb, s, half] = acc

    parity = lax.rem(n, 2)
    pl.when(parity == 0)(lambda: step(z0_ref, z1_ref))
    pl.when(parity == 1)(lambda: step(z1_ref, z0_ref))


def _ffn_up(x, norm_w, w_up, conv_w, conv_b, l, tm):
    T, D = x.shape
    n_tiles = pl.cdiv(D_FF, FF_TILE)
    n_steps = (T // tm) * n_tiles
    per = tm // HALO
    last = T // HALO - 1
    El = pl.Element

    def row(n):
        return jnp.minimum(n // n_tiles, T // tm - 1)

    def col(n):
        return jnp.where(n == n_steps, n_tiles - 1, lax.rem(n, n_tiles))

    def prev(n):
        return jnp.maximum(n - 1, 0)

    def off(j, base=0):
        return (base // LANE + jnp.minimum(j * (FF_TILE // LANE), (D_FF - FF_TILE) // LANE)) * LANE

    def halves(rows, which):
        return [pl.BlockSpec((None, El(rows), El(FF_TILE)), lambda n: (l, 0, off(which(n)))),
                pl.BlockSpec((None, El(rows), El(FF_TILE)), lambda n: (l, 0, off(which(n), D_FF)))]

    prev_col = lambda n: lax.rem(prev(n), n_tiles)
    z_shape = (tm // FF_MB, FF_TILE // FF_SUB, 2, FF_MB + 2 * HALO, FF_SUB)
    return pl.pallas_call(
        functools.partial(_ffn_up_kernel, tm=tm, n_tiles=n_tiles, n_steps=n_steps),
        grid=(n_steps + 1,),
        in_specs=[
            pl.BlockSpec((tm, D), lambda n: (row(n), 0)),
            pl.BlockSpec((HALO, D), lambda n: (jnp.maximum(row(n) * per - 1, 0), 0)),
            pl.BlockSpec((HALO, D), lambda n: (jnp.minimum((row(n) + 1) * per, last), 0)),
            pl.BlockSpec((1, D), lambda n: (0, 0)),
        ] + halves(D, col) + halves(3, prev_col) + halves(1, prev_col),
        out_specs=pl.BlockSpec((tm, FF_TILE), lambda n: (prev(n) // n_tiles, prev_col(n))),
        out_shape=jax.ShapeDtypeStruct((T, n_tiles * FF_TILE), BF16),
        scratch_shapes=[pltpu.VMEM((tm + 2 * HALO, D), BF16), pltpu.VMEM(z_shape, F32), pltpu.VMEM(z_shape, F32)],
        compiler_params=_params(("arbitrary",)),
        name="ffn_up",
    )(x, x, x, norm_w.reshape(1, D), w_up, w_up, conv_w, conv_w, conv_b, conv_b)


def _ffn_down_kernel(x_ref, a_ref, w_ref, out_ref):
    full = (D_FF // FF_TILE) * FF_TILE
    acc = _dot(a_ref[:, :full], w_ref[:full, :]) + _dot(a_ref[:, full + FF_DUP:], w_ref[full:, :])
    out_ref[...] = x_ref[...] + acc


def _ffn_down(x, act, w_down, l, tm, tn):
    T, D = x.shape
    K = act.shape[1]
    return pl.pallas_call(
        _ffn_down_kernel,
        grid=(T // tm, D // tn),
        in_specs=[
            pl.BlockSpec((tm, tn), lambda i, j: (i, j)),
            pl.BlockSpec((tm, K), lambda i, j: (i, 0)),
            pl.BlockSpec((None, D_FF, tn), lambda i, j: (l, 0, j)),
        ],
        out_specs=pl.BlockSpec((tm, tn), lambda i, j: (i, j)),
        out_shape=jax.ShapeDtypeStruct((T, D), F32),
        compiler_params=_params(("arbitrary", "arbitrary")),
        name="ffn_down",
    )(x, act, w_down)


def _final_norm_kernel(x_ref, w_ref, out_ref):
    xf = x_ref[...]
    ms = jnp.mean(xf * xf, axis=-1, keepdims=True)
    out_ref[...] = xf * lax.rsqrt(ms + EPS) * w_ref[...]


def _final_norm(x, w, tm):
    T, D = x.shape
    return pl.pallas_call(
        _final_norm_kernel,
        grid=(T // tm,),
        in_specs=[pl.BlockSpec((tm, D), lambda i: (i, 0)), pl.BlockSpec((1, D), lambda i: (0, 0))],
        out_specs=pl.BlockSpec((tm, D), lambda i: (i, 0)),
        out_shape=jax.ShapeDtypeStruct((T, D), F32),
        compiler_params=_params(("arbitrary",)),
        name="final_norm",
    )(x, w.reshape(1, D))


def _split_in_proj(w_in):
    depth, D, d_in = w_in.shape
    g0, g1 = GATE_SLICE
    n_main = d_in - (g1 - g0)
    rows = 256

    def body(w_ref, main_ref, gate_ref):
        w = w_ref[...]
        main_ref[:, :g0] = w[:, :g0].astype(BF16)
        main_ref[:, g0:] = w[:, g1:].astype(BF16)
        gate = jnp.concatenate([w[:, g0:g1], jnp.zeros((rows, GATE_PAD - (g1 - g0)), F32)], axis=1)
        gate_ref[...] = gate.astype(BF16)

    return pl.pallas_call(
        body,
        grid=(depth, D // rows),
        in_specs=[pl.BlockSpec((None, rows, d_in), lambda l, r: (l, r, 0))],
        out_specs=[pl.BlockSpec((None, rows, n_main), lambda l, r: (l, r, 0)),
                   pl.BlockSpec((None, rows, GATE_PAD), lambda l, r: (l, r, 0))],
        out_shape=[jax.ShapeDtypeStruct((depth, D, n_main), BF16),
                   jax.ShapeDtypeStruct((depth, D, GATE_PAD), BF16)],
        compiler_params=_params(("arbitrary", "arbitrary")),
        name="w_in_prep",
    )(w_in)


def _tile(T, want):
    return min(want, T)


def kernel(x, norm1_w, w_in, mlstm_conv_w, mlstm_conv_b, mlstm_i_bias, mlstm_f_bias, mlstm_norm_w,
           hgrn_lb_logits, hgrn_f_bias, hgrn_norm_w, sgu_ln_w, sgu_ln_b, sgu_w, sgu_b, na_rel_bias,
           out_norm_w, w_out, norm2_w, ffn_w_up, ffn_conv_w, ffn_conv_b, ffn_w_down, final_norm_w):
    B, T, D = x.shape
    assert B == 1 and D == D_MODEL and T % (NA_ROWS * GRID_W) == 0
    depth = w_in.shape[0]
    tm = _tile(T, 1024)
    G = _tile(T, 512)

    lb_all = jax.nn.softmax(hgrn_lb_logits.astype(F32), axis=0)
    lower_bounds = jnp.cumsum(lb_all, axis=0) - lb_all[0]

    w_main, w_gate = _split_in_proj(w_in)
    w_out_b = w_out.astype(BF16)
    w_up = ffn_w_up.astype(BF16)
    conv_w = ffn_conv_w.astype(F32)
    conv_b = ffn_conv_b.astype(F32)[:, None, :]
    w_dn = ffn_w_down.astype(BF16)
    bands = _na_bias_bands(na_rel_bias)

    xs = x[0]
    for l in range(depth):
        proj, proj_b, gates = _proj_in(xs, norm1_w[l], w_main, w_gate, l, tm)
        y_a = _mlstm(proj, proj_b, gates, mlstm_conv_w[l], mlstm_conv_b[l], mlstm_i_bias[l], mlstm_f_bias[l],
                     mlstm_norm_w[l], G)
        y_b = _hgrn(proj, proj_b, lower_bounds[l], hgrn_f_bias[l], hgrn_norm_w[l], G)
        y_c = _sgu(proj, sgu_ln_w[l], sgu_ln_b[l], sgu_w[l], sgu_b[l], out_norm_w[l, :W_GRP], G)
        y_d = _na(proj_b, bands, out_norm_w[l, W_GRP:], l, min(NA_ROW_BLOCK, T // GRID_W))
        xs = _proj_out(xs, (y_a, y_b, y_c, y_d), w_out_b, l, tm, 1024)
        act = _ffn_up(xs, norm2_w[l], w_up, conv_w, conv_b, l, tm)
        xs = _ffn_down(xs, act, w_dn, l, tm, 512)
    return _final_norm(xs, final_norm_w, tm)[None]
```
